```python
import math
import jax, jax.numpy as jnp
from jax import lax
import numpy as np

D_MODEL = 1024
BATCH = 2
SEQ = 8192
DEPTH = 1

GRID_W = 64
HEAD_DIM = 64
N_Q_HEADS = 8
N_KV_HEADS = 2
GQA_GROUP = N_Q_HEADS // N_KV_HEADS
ATTN_WIDTH = N_Q_HEADS * HEAD_DIM
KV_WIDTH = N_KV_HEADS * HEAD_DIM
Q_BLOCK = 128
AXIAL_DIM = HEAD_DIM // 2
ROPE_THETA = 10000.0
CHUNK = 128
N_GROUPS = 8
GMLP_WIDTH = 512
GROUP_DIM = GMLP_WIDTH // N_GROUPS
N_BRANCHES = 2
EPS = 1e-6

IN_SIZES = (ATTN_WIDTH, KV_WIDTH, KV_WIDTH, ATTN_WIDTH,
            GMLP_WIDTH, GMLP_WIDTH, GMLP_WIDTH, N_BRANCHES * D_MODEL)
IN_WIDTH = sum(IN_SIZES)
IN_SPLITS = tuple(int(s) for s in np.cumsum(IN_SIZES)[:-1])

kernel_name = "hybrid_gqa_axialrope_gmlp_gated_merge"


def rmsnorm(x, gain):
    x32 = x.astype(jnp.float32)
    y = x32 * lax.rsqrt(jnp.mean(x32 * x32, axis=-1, keepdims=True) + EPS)
    return (y * gain.astype(jnp.float32)).astype(x.dtype)


def layernorm(x, gain, bias):
    x32 = x.astype(jnp.float32)
    mu = jnp.mean(x32, axis=-1, keepdims=True)
    var = jnp.mean(jnp.square(x32 - mu), axis=-1, keepdims=True)
    y = (x32 - mu) * lax.rsqrt(var + EPS)
    return (y * gain.astype(jnp.float32) + bias.astype(jnp.float32)).astype(x.dtype)


def rope_half(x, ang):
    n = x.shape[-1] // 2
    x1, x2 = x[..., :n], x[..., n:]
    cos, sin = jnp.cos(ang).astype(x.dtype), jnp.sin(ang).astype(x.dtype)
    return jnp.concatenate([x1 * cos - x2 * sin, x1 * sin + x2 * cos], axis=-1)


def axial_rope(x, row_ang, col_ang):
    return jnp.concatenate([rope_half(x[..., :AXIAL_DIM], row_ang),
                            rope_half(x[..., AXIAL_DIM:], col_ang)], axis=-1)


def block_attention(q, k, v):
    B, S = q.shape[0], q.shape[1]
    n_blk = S // Q_BLOCK
    scale = 1.0 / math.sqrt(HEAD_DIM)
    qb = q.reshape(B, n_blk, Q_BLOCK, N_KV_HEADS, GQA_GROUP, HEAD_DIM).transpose(1, 0, 3, 4, 2, 5)
    kt = k.transpose(0, 2, 1, 3)
    vt = v.transpose(0, 2, 1, 3)

    def one_block(q_blk):
        s = jnp.einsum('bhgqd,bhkd->bhgqk', q_blk, kt).astype(jnp.float32) * scale
        p = jax.nn.softmax(s, axis=-1).astype(vt.dtype)
        return jnp.einsum('bhgqk,bhkd->bhgqd', p, vt)

    o = lax.map(one_block, qb)
    return o.transpose(1, 0, 4, 2, 3, 5).reshape(B, S, ATTN_WIDTH)


def hybrid_layer(x, norm_gain, w_in, q_gain, k_gain, w_proj_attn,
                 ln_v_gain, ln_v_bias, w_spatial, b_spatial, w_proj_gmlp,
                 b_merge, w_out):
    B, S, _ = x.shape
    rows = S // GRID_W
    h = rmsnorm(x, norm_gain)
    z = jnp.einsum('bsd,de->bse', h, w_in)
    q, k, v, gate_a, u_in, v_in, gate_b, g_merge = jnp.split(z, IN_SPLITS, axis=-1)

    row_idx = jnp.repeat(jnp.arange(rows, dtype=jnp.float32), GRID_W)
    col_idx = jnp.tile(jnp.arange(GRID_W, dtype=jnp.float32), rows)
    inv_freq = ROPE_THETA ** (-jnp.arange(0, AXIAL_DIM, 2, dtype=jnp.float32) / AXIAL_DIM)
    row_ang = (row_idx[:, None] * inv_freq)[:, None, :]
    col_ang = (col_idx[:, None] * inv_freq)[:, None, :]
    q = axial_rope(rmsnorm(q.reshape(B, S, N_Q_HEADS, HEAD_DIM), q_gain), row_ang, col_ang)
    k = axial_rope(rmsnorm(k.reshape(B, S, N_KV_HEADS, HEAD_DIM), k_gain), row_ang, col_ang)
    v = v.reshape(B, S, N_KV_HEADS, HEAD_DIM)
    attn = block_attention(q, k, v) * jax.nn.silu(gate_a)
    y_a = jnp.einsum('bse,ed->bsd', attn, w_proj_attn)

    u = jax.nn.gelu(u_in)
    vv = layernorm(jax.nn.gelu(v_in), ln_v_gain, ln_v_bias)
    vv = vv.reshape(B, S // CHUNK, CHUNK, N_GROUPS, GROUP_DIM)
    v_mix = jnp.einsum('gij,bcjgd->bcigd', w_spatial, vv) + b_spatial.T[:, :, None]
    gm = u * v_mix.reshape(B, S, GMLP_WIDTH) * jax.nn.silu(gate_b)
    y_b = jnp.einsum('bse,ed->bsd', gm, w_proj_gmlp)

    g = jax.nn.sigmoid(g_merge.reshape(B, S, N_BRANCHES, D_MODEL) + b_merge)
    y = g[:, :, 0, :] * y_a + g[:, :, 1, :] * y_b
    return x + jnp.einsum('bsd,de->bse', y, w_out)


def setup_inputs(seed: int = 0) -> dict:
    key = jax.random.key(seed)
    ks = jax.random.split(key, 16)
    L = DEPTH
    nrm = lambda k, shape, fan: jax.random.normal(k, shape, jnp.float32) * (fan ** -0.5)
    return {
        "x": jax.random.normal(ks[0], (BATCH, SEQ, D_MODEL), jnp.float32),
        "norm_gain": 1.0 + 0.05 * jax.random.normal(ks[1], (L, D_MODEL), jnp.float32),
        "w_in": nrm(ks[2], (L, D_MODEL, IN_WIDTH), D_MODEL),
        "q_gain": 1.0 + 0.05 * jax.random.normal(ks[3], (L, HEAD_DIM), jnp.float32),
        "k_gain": 1.0 + 0.05 * jax.random.normal(ks[4], (L, HEAD_DIM), jnp.float32),
        "w_proj_attn": nrm(ks[5], (L, ATTN_WIDTH, D_MODEL), ATTN_WIDTH),
        "ln_v_gain": 1.0 + 0.05 * jax.random.normal(ks[6], (L, GMLP_WIDTH), jnp.float32),
        "ln_v_bias": 0.02 * jax.random.normal(ks[7], (L, GMLP_WIDTH), jnp.float32),
        "w_spatial": nrm(ks[8], (L, N_GROUPS, CHUNK, CHUNK), CHUNK),
        "b_spatial": 1.0 + 0.05 * jax.random.normal(ks[9], (L, N_GROUPS, CHUNK), jnp.float32),
        "w_proj_gmlp": nrm(ks[10], (L, GMLP_WIDTH, D_MODEL), GMLP_WIDTH),
        "b_merge": 0.02 * jax.random.normal(ks[11], (L, N_BRANCHES, D_MODEL), jnp.float32),
        "w_out": nrm(ks[12], (L, D_MODEL, D_MODEL), D_MODEL),
        "final_gain": 1.0 + 0.05 * jax.random.normal(ks[13], (D_MODEL,), jnp.float32),
    }


def reference(x, norm_gain, w_in, q_gain, k_gain, w_proj_attn, ln_v_gain, ln_v_bias,
              w_spatial, b_spatial, w_proj_gmlp, b_merge, w_out, final_gain):
    for l in range(DEPTH):
        x = hybrid_layer(x, norm_gain[l], w_in[l], q_gain[l], k_gain[l], w_proj_attn[l],
                         ln_v_gain[l], ln_v_bias[l], w_spatial[l], b_spatial[l],
                         w_proj_gmlp[l], b_merge[l], w_out[l])
    return rmsnorm(x, final_gain)
```

```python
import functools
import math

import jax
import jax.numpy as jnp
import numpy as np
from jax.experimental import pallas as pl
from jax.experimental.pallas import tpu as pltpu

D_MODEL = 1024
GRID_W = 64
HEAD_DIM = 64
N_Q_HEADS = 8
N_KV_HEADS = 2
GQA_GROUP = N_Q_HEADS // N_KV_HEADS
ATTN_WIDTH = N_Q_HEADS * HEAD_DIM
KV_WIDTH = N_KV_HEADS * HEAD_DIM
AXIAL_DIM = HEAD_DIM // 2
ROPE_THETA = 10000.0
CHUNK = 128
N_GROUPS = 8
GMLP_WIDTH = 512
GROUP_DIM = GMLP_WIDTH // N_GROUPS
N_BRANCHES = 2
EPS = 1e-6
REST_WIDTH = ATTN_WIDTH + 3 * GMLP_WIDTH + N_BRANCHES * D_MODEL

LANES = 128
VMEM_LIMIT_BYTES = 56 * 1024 * 1024

TM_IN = 512
TQ = 256
CK = 512
TM_POST = 512
NEG_BIG = -1e30


def _segment_sum_mxu(sq, ones_bd):
    hi = sq.astype(jnp.bfloat16)
    lo = (sq - hi.astype(jnp.float32)).astype(jnp.bfloat16)
    return (jnp.dot(hi, ones_bd, preferred_element_type=jnp.float32)
            + jnp.dot(lo, ones_bd, preferred_element_type=jnp.float32))


def _head_norm_rope(z, gain, cos, sin_signed, ones_bd, scale):
    ms = _segment_sum_mxu(z * z, ones_bd) * (1.0 / HEAD_DIM)
    y = z * (jax.lax.rsqrt(ms + EPS) * scale) * gain
    lane = jax.lax.broadcasted_iota(jnp.int32, y.shape, 1)
    half = AXIAL_DIM // 2
    partner = jnp.where((lane & half) != 0,
                        pltpu.roll(y, half, 1),
                        pltpu.roll(y, LANES - half, 1))
    return y * cos + partner * sin_signed


def _in_proj_kernel(x_ref, ng_ref, w_ref, qg_ref, kg_ref, cos_ref, sin_ref, bd_ref,
                    q_ref, kx_ref, vx_ref, rest_ref):
    x = x_ref[...]
    ms = jnp.mean(x * x, axis=-1, keepdims=True)
    h = (x * jax.lax.rsqrt(ms + EPS) * ng_ref[...]).astype(jnp.bfloat16)
    cos = cos_ref[...]
    sin_signed = sin_ref[...]
    ones_bd = bd_ref[...]
    lane = jax.lax.broadcasted_iota(jnp.int32, (x.shape[0], LANES), 1)
    low = lane < HEAD_DIM

    scale = 1.0 / math.sqrt(HEAD_DIM)
    for j in range(ATTN_WIDTH // LANES):
        z = jnp.dot(h, w_ref[:, j * LANES:(j + 1) * LANES],
                    preferred_element_type=jnp.float32)
        q = _head_norm_rope(z, qg_ref[...], cos, sin_signed, ones_bd, scale)
        q_ref[:, j * LANES:(j + 1) * LANES] = q.astype(q_ref.dtype)

    z = jnp.dot(h, w_ref[:, ATTN_WIDTH:ATTN_WIDTH + KV_WIDTH],
                preferred_element_type=jnp.float32)
    k = _head_norm_rope(z, kg_ref[...], cos, sin_signed, ones_bd, 1.0)
    k_sw = pltpu.roll(k, HEAD_DIM, 1)
    zero = jnp.zeros_like(k)
    kx_ref[:, 0 * LANES:1 * LANES] = jnp.where(low, k, zero).astype(kx_ref.dtype)
    kx_ref[:, 1 * LANES:2 * LANES] = jnp.where(low, zero, k_sw).astype(kx_ref.dtype)
    kx_ref[:, 2 * LANES:3 * LANES] = jnp.where(low, k_sw, zero).astype(kx_ref.dtype)
    kx_ref[:, 3 * LANES:4 * LANES] = jnp.where(low, zero, k).astype(kx_ref.dtype)

    v = jnp.dot(h, w_ref[:, ATTN_WIDTH + KV_WIDTH:ATTN_WIDTH + 2 * KV_WIDTH],
                preferred_element_type=jnp.float32)
    v_sw = pltpu.roll(v, HEAD_DIM, 1)
    ones_col = jnp.where(lane == HEAD_DIM, 1.0, 0.0)
    vx_ref[:, 0 * LANES:1 * LANES] = jnp.where(low, v, ones_col).astype(vx_ref.dtype)
    vx_ref[:, 1 * LANES:2 * LANES] = jnp.where(low, v_sw, ones_col).astype(vx_ref.dtype)

    base = ATTN_WIDTH + 2 * KV_WIDTH
    step = 512
    for c in range(REST_WIDTH // step):
        z = jnp.dot(h, w_ref[:, base + c * step:base + (c + 1) * step],
                    preferred_element_type=jnp.float32)
        rest_ref[:, c * step:(c + 1) * step] = z.astype(rest_ref.dtype)


def _attn_kernel(q_ref, kx_ref, vx_ref, o_ref, m_ref, acc_ref):
    n_heads = GQA_GROUP
    tq = q_ref.shape[0]
    seq = kx_ref.shape[0]
    m_ref[...] = jnp.full(m_ref.shape, NEG_BIG, jnp.float32)
    acc_ref[...] = jnp.zeros(acc_ref.shape, jnp.float32)

    def body(c, carry):
        start = pl.multiple_of(c * CK, CK)
        vx = vx_ref[pl.ds(start, CK), :]
        for h in range(n_heads):
            qp = q_ref[:, (h // 2) * LANES:(h // 2 + 1) * LANES]
            kv = kx_ref[pl.ds(start, CK), (h % 2) * LANES:(h % 2 + 1) * LANES]
            s = jax.lax.dot_general(qp, kv, (((1,), (1,)), ((), ())),
                                    preferred_element_type=jnp.float32)
            m_old = m_ref[h]
            m_new = jnp.maximum(m_old, jnp.max(s, axis=-1, keepdims=True))
            alpha = jnp.exp(m_old - m_new)
            p = jnp.exp(s - m_new[:, :1]).astype(jnp.bfloat16)
            acc_ref[h] = acc_ref[h] * alpha + jnp.dot(
                p, vx, preferred_element_type=jnp.float32)
            m_ref[h] = m_new
        return carry

    jax.lax.fori_loop(0, seq // CK, body, 0)

    for h in range(n_heads):
        acc = acc_ref[h]
        o = acc[:, :HEAD_DIM] / acc[:, HEAD_DIM:HEAD_DIM + 1]
        o_ref[:, h * HEAD_DIM:(h + 1) * HEAD_DIM] = o.astype(o_ref.dtype)


def _post_kernel(attn_ref, rest_ref, x_ref, wpa_ref, wpg_ref, wout_ref, ws_ref,
                 bsp_ref, lng_ref, lnb_ref, bm_ref, fg_ref, o_ref, vmix_ref):
    f32 = jnp.float32
    tm = x_ref.shape[0]

    def rest(lo, hi):
        return rest_ref[:, lo:hi].astype(f32)

    a = attn_ref[...] * jax.nn.silu(rest(0, ATTN_WIDTH))
    y_a = jnp.dot(a.astype(jnp.bfloat16), wpa_ref[...], preferred_element_type=f32)

    o0 = ATTN_WIDTH
    vg = jax.nn.gelu(rest(o0 + GMLP_WIDTH, o0 + 2 * GMLP_WIDTH))
    mu = jnp.mean(vg, axis=-1, keepdims=True)
    var = jnp.mean(jnp.square(vg - mu), axis=-1, keepdims=True)
    vv = ((vg - mu) * jax.lax.rsqrt(var + EPS) * lng_ref[...] + lnb_ref[...]).astype(jnp.bfloat16)
    group = jax.lax.broadcasted_iota(jnp.int32, (CHUNK, GMLP_WIDTH), 1) // GROUP_DIM
    for c in range(tm // CHUNK):
        r = jnp.dot(ws_ref[...], vv[c * CHUNK:(c + 1) * CHUNK, :], preferred_element_type=f32)
        mix = r[0:CHUNK, :]
        for g in range(1, N_GROUPS):
            mix = jnp.where(group == g, r[g * CHUNK:(g + 1) * CHUNK, :], mix)
        vmix_ref[c * CHUNK:(c + 1) * CHUNK, :] = mix + bsp_ref[...]
    u = jax.nn.gelu(rest(o0, o0 + GMLP_WIDTH))
    gm = u * vmix_ref[...] * jax.nn.silu(rest(o0 + 2 * GMLP_WIDTH, o0 + 3 * GMLP_WIDTH))
    y_b = jnp.dot(gm.astype(jnp.bfloat16), wpg_ref[...], preferred_element_type=f32)

    o1 = o0 + 3 * GMLP_WIDTH
    g0 = jax.nn.sigmoid(rest(o1, o1 + D_MODEL) + bm_ref[:, 0:D_MODEL])
    g1 = jax.nn.sigmoid(rest(o1 + D_MODEL, o1 + 2 * D_MODEL) + bm_ref[:, D_MODEL:2 * D_MODEL])
    y = g0 * y_a + g1 * y_b
    out = x_ref[...] + jnp.dot(y.astype(jnp.bfloat16), wout_ref[...], preferred_element_type=f32)
    ms = jnp.mean(out * out, axis=-1, keepdims=True)
    o_ref[...] = out * jax.lax.rsqrt(ms + EPS) * fg_ref[...]


def _rope_tables(seq):
    rows = seq // GRID_W
    row_idx = jnp.repeat(jnp.arange(rows, dtype=jnp.float32), GRID_W)
    col_idx = jnp.tile(jnp.arange(GRID_W, dtype=jnp.float32), rows)
    inv_freq = ROPE_THETA ** (-jnp.arange(0, AXIAL_DIM, 2, dtype=jnp.float32) / AXIAL_DIM)
    row_ang = row_idx[:, None] * inv_freq
    col_ang = col_idx[:, None] * inv_freq
    cos = jnp.concatenate([jnp.cos(row_ang)] * 2 + [jnp.cos(col_ang)] * 2, axis=-1)
    sin = jnp.concatenate([-jnp.sin(row_ang), jnp.sin(row_ang),
                           -jnp.sin(col_ang), jnp.sin(col_ang)], axis=-1)
    return jnp.tile(cos, (1, 2)), jnp.tile(sin, (1, 2))


def _layer(x2, batch, seq, norm_gain, w_in, q_gain, k_gain, w_proj_attn, ln_v_gain, ln_v_bias,
           w_spatial, b_spatial, w_proj_gmlp, b_merge, w_out, final_gain):
    n = batch * seq
    bf16 = jnp.bfloat16
    f32 = jnp.float32
    cos, sin = _rope_tables(seq)
    seg = np.arange(LANES) // HEAD_DIM
    ones_bd = jnp.asarray(seg[:, None] == seg[None, :], dtype=bf16)
    row = lambda v: v.reshape(1, -1).astype(f32)
    const = lambda shape: pl.BlockSpec(shape, lambda *_: (0,) * len(shape))
    in_width = w_in.shape[1]
    seq_blocks = seq // TM_IN

    q, kx, vx, rest = pl.pallas_call(
        _in_proj_kernel,
        grid=(n // TM_IN,),
        in_specs=[
            pl.BlockSpec((TM_IN, D_MODEL), lambda i: (i, 0)),
            const((1, D_MODEL)),
            const((D_MODEL, in_width)),
            const((1, LANES)),
            const((1, LANES)),
            pl.BlockSpec((TM_IN, LANES), lambda i: (i % seq_blocks, 0)),
            pl.BlockSpec((TM_IN, LANES), lambda i: (i % seq_blocks, 0)),
            const((LANES, LANES)),
        ],
        out_specs=[
            pl.BlockSpec((TM_IN, ATTN_WIDTH), lambda i: (i, 0)),
            pl.BlockSpec((TM_IN, 4 * LANES), lambda i: (i, 0)),
            pl.BlockSpec((TM_IN, 2 * LANES), lambda i: (i, 0)),
            pl.BlockSpec((TM_IN, REST_WIDTH), lambda i: (i, 0)),
        ],
        out_shape=[
            jax.ShapeDtypeStruct((n, ATTN_WIDTH), bf16),
            jax.ShapeDtypeStruct((n, 4 * LANES), bf16),
            jax.ShapeDtypeStruct((n, 2 * LANES), bf16),
            jax.ShapeDtypeStruct((n, REST_WIDTH), bf16),
        ],
        compiler_params=pltpu.CompilerParams(
            dimension_semantics=("arbitrary",), vmem_limit_bytes=VMEM_LIMIT_BYTES),
        name="in_proj",
    )(x2, row(norm_gain), w_in.astype(bf16), row(jnp.tile(q_gain, 2)), row(jnp.tile(k_gain, 2)),
      cos, sin, ones_bd)

    nq = seq // TQ
    kv_width = GQA_GROUP * HEAD_DIM
    attn = pl.pallas_call(
        _attn_kernel,
        grid=(batch, N_KV_HEADS, nq),
        in_specs=[
            pl.BlockSpec((TQ, kv_width), lambda b, g, i: (b * nq + i, g)),
            pl.BlockSpec((seq, 2 * LANES), lambda b, g, i: (b, g)),
            pl.BlockSpec((seq, LANES), lambda b, g, i: (b, g)),
        ],
        out_specs=pl.BlockSpec((TQ, kv_width), lambda b, g, i: (b * nq + i, g)),
        out_shape=jax.ShapeDtypeStruct((n, ATTN_WIDTH), f32),
        scratch_shapes=[
            pltpu.VMEM((GQA_GROUP, TQ, LANES), f32),
            pltpu.VMEM((GQA_GROUP, TQ, LANES), f32),
        ],
        compiler_params=pltpu.CompilerParams(
            dimension_semantics=("arbitrary", "arbitrary", "arbitrary"),
            vmem_limit_bytes=VMEM_LIMIT_BYTES),
        name="attention",
    )(q, kx, vx)

    ws_stack = w_spatial.reshape(N_GROUPS * CHUNK, CHUNK).astype(bf16)
    b_sp = jnp.repeat(b_spatial.T, GROUP_DIM, axis=1).astype(f32)
    out = pl.pallas_call(
        _post_kernel,
        grid=(n // TM_POST,),
        in_specs=[
            pl.BlockSpec((TM_POST, ATTN_WIDTH), lambda i: (i, 0)),
            pl.BlockSpec((TM_POST, REST_WIDTH), lambda i: (i, 0)),
            pl.BlockSpec((TM_POST, D_MODEL), lambda i: (i, 0)),
            const((ATTN_WIDTH, D_MODEL)),
            const((GMLP_WIDTH, D_MODEL)),
            const((D_MODEL, D_MODEL)),
            const((N_GROUPS * CHUNK, CHUNK)),
            const((CHUNK, GMLP_WIDTH)),
            const((1, GMLP_WIDTH)),
            const((1, GMLP_WIDTH)),
            const((1, N_BRANCHES * D_MODEL)),
            const((1, D_MODEL)),
        ],
        out_specs=pl.BlockSpec((TM_POST, D_MODEL), lambda i: (i, 0)),
        out_shape=jax.ShapeDtypeStruct((n, D_MODEL), f32),
        scratch_shapes=[pltpu.VMEM((TM_POST, GMLP_WIDTH), f32)],
        compiler_params=pltpu.CompilerParams(
            dimension_semantics=("arbitrary",), vmem_limit_bytes=VMEM_LIMIT_BYTES),
        name="post",
    )(attn, rest, x2, w_proj_attn.astype(bf16), w_proj_gmlp.astype(bf16), w_out.astype(bf16),
      ws_stack, b_sp, row(ln_v_gain), row(ln_v_bias), row(b_merge), row(final_gain))
    return out


def kernel(x, norm_gain, w_in, q_gain, k_gain, w_proj_attn, ln_v_gain, ln_v_bias, w_spatial,
           b_spatial, w_proj_gmlp, b_merge, w_out, final_gain):
    batch, seq, d = x.shape
    depth = w_in.shape[0]
    assert depth == 1, "the final rmsnorm is fused into the single layer's last call"
    assert d == D_MODEL and seq % TM_IN == 0 and seq % TQ == 0 and seq % CK == 0
    out = _layer(x.reshape(batch * seq, d), batch, seq, norm_gain[0], w_in[0], q_gain[0],
                 k_gain[0], w_proj_attn[0], ln_v_gain[0], ln_v_bias[0], w_spatial[0],
                 b_spatial[0], w_proj_gmlp[0], b_merge[0], w_out[0], final_gain)
    return out.reshape(batch, seq, d)
```

```python
import functools
import math

import jax
import jax.numpy as jnp
import numpy as np
from jax.experimental import pallas as pl
from jax.experimental.pallas import tpu as pltpu

D_MODEL = 1024
GRID_W = 64
HEAD_DIM = 64
N_Q_HEADS = 8
N_KV_HEADS = 2
GQA_GROUP = N_Q_HEADS // N_KV_HEADS
ATTN_WIDTH = N_Q_HEADS * HEAD_DIM
KV_WIDTH = N_KV_HEADS * HEAD_DIM
AXIAL_DIM = HEAD_DIM // 2
ROPE_THETA = 10000.0
CHUNK = 128
N_GROUPS = 8
GMLP_WIDTH = 512
GROUP_DIM = GMLP_WIDTH // N_GROUPS
N_BRANCHES = 2
EPS = 1e-6
REST_WIDTH = ATTN_WIDTH + 3 * GMLP_WIDTH + N_BRANCHES * D_MODEL

LANES = 128
VMEM_LIMIT_BYTES = 56 * 1024 * 1024

TM_IN = 512
TQ = 256
CK = 512
TM_POST = 512
NEG_BIG = -1e30


def _segment_sum_mxu(sq, ones_bd):
    hi = sq.astype(jnp.bfloat16)
    lo = (sq - hi.astype(jnp.float32)).astype(jnp.bfloat16)
    return (jnp.dot(hi, ones_bd, preferred_element_type=jnp.float32)
            + jnp.dot(lo, ones_bd, preferred_element_type=jnp.float32))


def _head_norm_rope(z, gain, cos, sin_signed, ones_bd, scale):
    ms = _segment_sum_mxu(z * z, ones_bd) * (1.0 / HEAD_DIM)
    y = z * (jax.lax.rsqrt(ms + EPS) * scale) * gain
    lane = jax.lax.broadcasted_iota(jnp.int32, y.shape, 1)
    half = AXIAL_DIM // 2
    partner = jnp.where((lane & half) != 0,
                        pltpu.roll(y, half, 1),
                        pltpu.roll(y, LANES - half, 1))
    return y * cos + partner * sin_signed


def _in_proj_kernel(x_ref, ng_ref, w_ref, qg_ref, kg_ref, cos_ref, sin_ref, bd_ref,
                    qt_ref, kx_ref, vxt_ref, rest_ref):
    x = x_ref[...]
    ms = jnp.mean(x * x, axis=-1, keepdims=True)
    h = (x * jax.lax.rsqrt(ms + EPS) * ng_ref[...]).astype(jnp.bfloat16)
    cos = cos_ref[...]
    sin_signed = sin_ref[...]
    ones_bd = bd_ref[...]
    lane = jax.lax.broadcasted_iota(jnp.int32, (x.shape[0], LANES), 1)
    low = lane < HEAD_DIM

    scale = math.log2(math.e) / math.sqrt(HEAD_DIM)
    for j in range(ATTN_WIDTH // LANES):
        z = jnp.dot(h, w_ref[:, j * LANES:(j + 1) * LANES],
                    preferred_element_type=jnp.float32)
        q = _head_norm_rope(z, qg_ref[...], cos, sin_signed, ones_bd, scale)
        qt_ref[j * LANES:(j + 1) * LANES, :] = q.T.astype(qt_ref.dtype)

    z = jnp.dot(h, w_ref[:, ATTN_WIDTH:ATTN_WIDTH + KV_WIDTH],
                preferred_element_type=jnp.float32)
    k = _head_norm_rope(z, kg_ref[...], cos, sin_signed, ones_bd, 1.0)
    k_sw = pltpu.roll(k, HEAD_DIM, 1)
    zero = jnp.zeros_like(k)
    kx_ref[:, 0 * LANES:1 * LANES] = jnp.where(low, k, zero).astype(kx_ref.dtype)
    kx_ref[:, 1 * LANES:2 * LANES] = jnp.where(low, zero, k_sw).astype(kx_ref.dtype)
    kx_ref[:, 2 * LANES:3 * LANES] = jnp.where(low, k_sw, zero).astype(kx_ref.dtype)
    kx_ref[:, 3 * LANES:4 * LANES] = jnp.where(low, zero, k).astype(kx_ref.dtype)

    v = jnp.dot(h, w_ref[:, ATTN_WIDTH + KV_WIDTH:ATTN_WIDTH + 2 * KV_WIDTH],
                preferred_element_type=jnp.float32)
    v_sw = pltpu.roll(v, HEAD_DIM, 1)
    ones_col = jnp.where(lane == HEAD_DIM, 1.0, 0.0)
    vxt_ref[0 * LANES:1 * LANES, :] = jnp.where(low, v, ones_col).T.astype(vxt_ref.dtype)
    vxt_ref[1 * LANES:2 * LANES, :] = jnp.where(low, v_sw, ones_col).T.astype(vxt_ref.dtype)

    base = ATTN_WIDTH + 2 * KV_WIDTH
    step = 512
    for c in range(REST_WIDTH // step):
        z = jnp.dot(h, w_ref[:, base + c * step:base + (c + 1) * step],
                    preferred_element_type=jnp.float32)
        rest_ref[:, c * step:(c + 1) * step] = z.astype(rest_ref.dtype)


def _attn_kernel(qt_ref, kx_ref, vxt_ref, o_ref, m_ref, acc_ref, s0_ref, s1_ref, mc0_ref, mc1_ref):
    n_heads = GQA_GROUP
    seq = kx_ref.shape[0]
    n_chunks = seq // CK
    m_ref[...] = jnp.full(m_ref.shape, NEG_BIG, jnp.float32)
    acc_ref[...] = jnp.zeros(acc_ref.shape, jnp.float32)

    def scores(c, s_ref, mc_ref):
        start = pl.multiple_of(c * CK, CK)
        for h in range(n_heads):
            qt = qt_ref[(h // 2) * LANES:(h // 2 + 1) * LANES, :]
            kv = kx_ref[pl.ds(start, CK), (h % 2) * LANES:(h % 2 + 1) * LANES]
            s = jnp.dot(kv, qt, preferred_element_type=jnp.float32)
            s_ref[h] = s
            mc_ref[h] = jnp.broadcast_to(jnp.max(s, axis=0, keepdims=True), mc_ref.shape[1:])

    def accumulate(c, s_ref, mc_ref):
        start = pl.multiple_of(c * CK, CK)
        vxt = vxt_ref[:, pl.ds(start, CK)]
        for h in range(n_heads):
            m_old = m_ref[h]
            m_new = jnp.maximum(m_old, mc_ref[h])
            alpha = jnp.exp2(m_old - m_new)
            p = jnp.exp2(s_ref[h] - m_new[0:1, :]).astype(jnp.bfloat16)
            acc_ref[h] = acc_ref[h] * alpha[0:1, :] + jnp.dot(
                vxt, p, preferred_element_type=jnp.float32)
            m_ref[h] = m_new

    scores(0, s0_ref, mc0_ref)

    def body(i, carry):
        c = 2 * i
        accumulate(c, s0_ref, mc0_ref)
        scores(c + 1, s1_ref, mc1_ref)
        accumulate(c + 1, s1_ref, mc1_ref)
        scores(c + 2, s0_ref, mc0_ref)
        return carry

    jax.lax.fori_loop(0, n_chunks // 2 - 1, body, 0)
    accumulate(n_chunks - 2, s0_ref, mc0_ref)
    scores(n_chunks - 1, s1_ref, mc1_ref)
    accumulate(n_chunks - 1, s1_ref, mc1_ref)

    lane = jax.lax.broadcasted_iota(jnp.int32, (o_ref.shape[0], LANES), 1)
    for j in range(n_heads // 2):
        halves = []
        for h in (2 * j, 2 * j + 1):
            acc = acc_ref[h]
            halves.append((acc / acc[HEAD_DIM:HEAD_DIM + 1, :]).T)
        pair = jnp.where(lane < HEAD_DIM, halves[0], pltpu.roll(halves[1], HEAD_DIM, 1))
        o_ref[:, j * LANES:(j + 1) * LANES] = pair.astype(o_ref.dtype)


def _post_kernel(attn_ref, rest_ref, x_ref, wpa_ref, wpg_ref, wout_ref, ws_ref,
                 bsp_ref, lng_ref, lnb_ref, bm_ref, fg_ref, o_ref, vmix_ref):
    f32 = jnp.float32
    tm = x_ref.shape[0]

    def rest(lo, hi):
        return rest_ref[:, lo:hi].astype(f32)

    a = attn_ref[...] * jax.nn.silu(rest(0, ATTN_WIDTH))
    y_a = jnp.dot(a.astype(jnp.bfloat16), wpa_ref[...], preferred_element_type=f32)

    o0 = ATTN_WIDTH
    vg = jax.nn.gelu(rest(o0 + GMLP_WIDTH, o0 + 2 * GMLP_WIDTH))
    mu = jnp.mean(vg, axis=-1, keepdims=True)
    var = jnp.mean(jnp.square(vg - mu), axis=-1, keepdims=True)
    vv = ((vg - mu) * jax.lax.rsqrt(var + EPS) * lng_ref[...] + lnb_ref[...]).astype(jnp.bfloat16)
    group = jax.lax.broadcasted_iota(jnp.int32, (CHUNK, GMLP_WIDTH), 1) // GROUP_DIM
    for c in range(tm // CHUNK):
        r = jnp.dot(ws_ref[...], vv[c * CHUNK:(c + 1) * CHUNK, :], preferred_element_type=f32)
        mix = r[0:CHUNK, :]
        for g in range(1, N_GROUPS):
            mix = jnp.where(group == g, r[g * CHUNK:(g + 1) * CHUNK, :], mix)
        vmix_ref[c * CHUNK:(c + 1) * CHUNK, :] = mix + bsp_ref[...]
    u = jax.nn.gelu(rest(o0, o0 + GMLP_WIDTH))
    gm = u * vmix_ref[...] * jax.nn.silu(rest(o0 + 2 * GMLP_WIDTH, o0 + 3 * GMLP_WIDTH))
    y_b = jnp.dot(gm.astype(jnp.bfloat16), wpg_ref[...], preferred_element_type=f32)

    o1 = o0 + 3 * GMLP_WIDTH
    g0 = jax.nn.sigmoid(rest(o1, o1 + D_MODEL) + bm_ref[:, 0:D_MODEL])
    g1 = jax.nn.sigmoid(rest(o1 + D_MODEL, o1 + 2 * D_MODEL) + bm_ref[:, D_MODEL:2 * D_MODEL])
    y = g0 * y_a + g1 * y_b
    out = x_ref[...] + jnp.dot(y.astype(jnp.bfloat16), wout_ref[...], preferred_element_type=f32)
    ms = jnp.mean(out * out, axis=-1, keepdims=True)
    o_ref[...] = out * jax.lax.rsqrt(ms + EPS) * fg_ref[...]


def _rope_tables(seq):
    rows = seq // GRID_W
    row_idx = jnp.repeat(jnp.arange(rows, dtype=jnp.float32), GRID_W)
    col_idx = jnp.tile(jnp.arange(GRID_W, dtype=jnp.float32), rows)
    inv_freq = ROPE_THETA ** (-jnp.arange(0, AXIAL_DIM, 2, dtype=jnp.float32) / AXIAL_DIM)
    row_ang = row_idx[:, None] * inv_freq
    col_ang = col_idx[:, None] * inv_freq
    cos = jnp.concatenate([jnp.cos(row_ang)] * 2 + [jnp.cos(col_ang)] * 2, axis=-1)
    sin = jnp.concatenate([-jnp.sin(row_ang), jnp.sin(row_ang),
                           -jnp.sin(col_ang), jnp.sin(col_ang)], axis=-1)
    return jnp.tile(cos, (1, 2)), jnp.tile(sin, (1, 2))


def _layer(x2, batch, seq, norm_gain, w_in, q_gain, k_gain, w_proj_attn, ln_v_gain, ln_v_bias,
           w_spatial, b_spatial, w_proj_gmlp, b_merge, w_out, final_gain):
    n = batch * seq
    bf16 = jnp.bfloat16
    f32 = jnp.float32
    cos, sin = _rope_tables(seq)
    seg = np.arange(LANES) // HEAD_DIM
    ones_bd = jnp.asarray(seg[:, None] == seg[None, :], dtype=bf16)
    row = lambda v: v.reshape(1, -1).astype(f32)
    const = lambda shape: pl.BlockSpec(shape, lambda *_: (0,) * len(shape))
    in_width = w_in.shape[1]
    seq_blocks = seq // TM_IN

    qt, kx, vxt, rest = pl.pallas_call(
        _in_proj_kernel,
        grid=(n // TM_IN,),
        in_specs=[
            pl.BlockSpec((TM_IN, D_MODEL), lambda i: (i, 0)),
            const((1, D_MODEL)),
            const((D_MODEL, in_width)),
            const((1, LANES)),
            const((1, LANES)),
            pl.BlockSpec((TM_IN, LANES), lambda i: (i % seq_blocks, 0)),
            pl.BlockSpec((TM_IN, LANES), lambda i: (i % seq_blocks, 0)),
            const((LANES, LANES)),
        ],
        out_specs=[
            pl.BlockSpec((ATTN_WIDTH, TM_IN), lambda i: (0, i)),
            pl.BlockSpec((TM_IN, 4 * LANES), lambda i: (i, 0)),
            pl.BlockSpec((2 * LANES, TM_IN), lambda i: (0, i)),
            pl.BlockSpec((TM_IN, REST_WIDTH), lambda i: (i, 0)),
        ],
        out_shape=[
            jax.ShapeDtypeStruct((ATTN_WIDTH, n), bf16),
            jax.ShapeDtypeStruct((n, 4 * LANES), bf16),
            jax.ShapeDtypeStruct((2 * LANES, n), bf16),
            jax.ShapeDtypeStruct((n, REST_WIDTH), bf16),
        ],
        compiler_params=pltpu.CompilerParams(
            dimension_semantics=("arbitrary",), vmem_limit_bytes=VMEM_LIMIT_BYTES),
        name="in_proj",
    )(x2, row(norm_gain), w_in.astype(bf16), row(jnp.tile(q_gain, 2)), row(jnp.tile(k_gain, 2)),
      cos, sin, ones_bd)

    nq = seq // TQ
    kv_width = GQA_GROUP * HEAD_DIM
    attn = pl.pallas_call(
        _attn_kernel,
        grid=(batch, N_KV_HEADS, nq),
        in_specs=[
            pl.BlockSpec((kv_width, TQ), lambda b, g, i: (g, b * nq + i)),
            pl.BlockSpec((seq, 2 * LANES), lambda b, g, i: (b, g)),
            pl.BlockSpec((LANES, seq), lambda b, g, i: (g, b)),
        ],
        out_specs=pl.BlockSpec((TQ, kv_width), lambda b, g, i: (b * nq + i, g)),
        out_shape=jax.ShapeDtypeStruct((n, ATTN_WIDTH), f32),
        scratch_shapes=[
            pltpu.VMEM((GQA_GROUP, 8, TQ), f32),
            pltpu.VMEM((GQA_GROUP, LANES, TQ), f32),
            pltpu.VMEM((GQA_GROUP, CK, TQ), f32),
            pltpu.VMEM((GQA_GROUP, CK, TQ), f32),
            pltpu.VMEM((GQA_GROUP, 8, TQ), f32),
            pltpu.VMEM((GQA_GROUP, 8, TQ), f32),
        ],
        compiler_params=pltpu.CompilerParams(
            dimension_semantics=("arbitrary", "arbitrary", "arbitrary"),
            vmem_limit_bytes=VMEM_LIMIT_BYTES),
        name="attention",
    )(qt, kx, vxt)

    ws_stack = w_spatial.reshape(N_GROUPS * CHUNK, CHUNK).astype(bf16)
    b_sp = jnp.repeat(b_spatial.T, GROUP_DIM, axis=1).astype(f32)
    out = pl.pallas_call(
        _post_kernel,
        grid=(n // TM_POST,),
        in_specs=[
            pl.BlockSpec((TM_POST, ATTN_WIDTH), lambda i: (i, 0)),
            pl.BlockSpec((TM_POST, REST_WIDTH), lambda i: (i, 0)),
            pl.BlockSpec((TM_POST, D_MODEL), lambda i: (i, 0)),
            const((ATTN_WIDTH, D_MODEL)),
            const((GMLP_WIDTH, D_MODEL)),
            const((D_MODEL, D_MODEL)),
            const((N_GROUPS * CHUNK, CHUNK)),
            const((CHUNK, GMLP_WIDTH)),
            const((1, GMLP_WIDTH)),
            const((1, GMLP_WIDTH)),
            const((1, N_BRANCHES * D_MODEL)),
            const((1, D_MODEL)),
        ],
        out_specs=pl.BlockSpec((TM_POST, D_MODEL), lambda i: (i, 0)),
        out_shape=jax.ShapeDtypeStruct((n, D_MODEL), f32),
        scratch_shapes=[pltpu.VMEM((TM_POST, GMLP_WIDTH), f32)],
        compiler_params=pltpu.CompilerParams(
            dimension_semantics=("arbitrary",), vmem_limit_bytes=VMEM_LIMIT_BYTES),
        name="post",
    )(attn, rest, x2, w_proj_attn.astype(bf16), w_proj_gmlp.astype(bf16), w_out.astype(bf16),
      ws_stack, b_sp, row(ln_v_gain), row(ln_v_bias), row(b_merge), row(final_gain))
    return out


def kernel(x, norm_gain, w_in, q_gain, k_gain, w_proj_attn, ln_v_gain, ln_v_bias, w_spatial,
           b_spatial, w_proj_gmlp, b_merge, w_out, final_gain):
    batch, seq, d = x.shape
    depth = w_in.shape[0]
    assert depth == 1, "the final rmsnorm is fused into the single layer's last call"
    assert d == D_MODEL and seq % TM_IN == 0 and seq % TQ == 0 and seq % CK == 0
    out = _layer(x.reshape(batch * seq, d), batch, seq, norm_gain[0], w_in[0], q_gain[0],
                 k_gain[0], w_proj_attn[0], ln_v_gain[0], ln_v_bias[0], w_spatial[0],
                 b_spatial[0], w_proj_gmlp[0], b_merge[0], w_out[0], final_gain)
    return out.reshape(batch, seq, d)
```

```python
import functools
import math

import jax
import jax.numpy as jnp
import numpy as np
from jax.experimental import pallas as pl
from jax.experimental.pallas import tpu as pltpu

D_MODEL = 1024
GRID_W = 64
HEAD_DIM = 64
N_Q_HEADS = 8
N_KV_HEADS = 2
GQA_GROUP = N_Q_HEADS // N_KV_HEADS
ATTN_WIDTH = N_Q_HEADS * HEAD_DIM
KV_WIDTH = N_KV_HEADS * HEAD_DIM
AXIAL_DIM = HEAD_DIM // 2
ROPE_THETA = 10000.0
CHUNK = 128
N_GROUPS = 8
GMLP_WIDTH = 512
GROUP_DIM = GMLP_WIDTH // N_GROUPS
N_BRANCHES = 2
EPS = 1e-6
REST_WIDTH = ATTN_WIDTH + 3 * GMLP_WIDTH + N_BRANCHES * D_MODEL

LANES = 128
VMEM_LIMIT_BYTES = 56 * 1024 * 1024

TM_IN = 512
TQ = 256
CK = 512
STEPS_PER_BODY = 6
V_ROWS = 80
TM_POST = 512
NEG_BIG = -1e30


def _segment_sum_mxu(sq, ones_bd):
    hi = sq.astype(jnp.bfloat16)
    lo = (sq - hi.astype(jnp.float32)).astype(jnp.bfloat16)
    return (jnp.dot(hi, ones_bd, preferred_element_type=jnp.float32)
            + jnp.dot(lo, ones_bd, preferred_element_type=jnp.float32))


def _head_norm_rope(z, gain, cos, sin_signed, ones_bd, scale):
    ms = _segment_sum_mxu(z * z, ones_bd) * (1.0 / HEAD_DIM)
    y = z * (jax.lax.rsqrt(ms + EPS) * scale) * gain
    lane = jax.lax.broadcasted_iota(jnp.int32, y.shape, 1)
    half = AXIAL_DIM // 2
    partner = jnp.where((lane & half) != 0,
                        pltpu.roll(y, half, 1),
                        pltpu.roll(y, LANES - half, 1))
    return y * cos + partner * sin_signed


def _in_proj_kernel(x_ref, ng_ref, w_ref, qg_ref, kg_ref, cos_ref, sin_ref, bd_ref,
                    qt_ref, kx_ref, vxt_ref, rest_ref):
    x = x_ref[...]
    ms = jnp.mean(x * x, axis=-1, keepdims=True)
    h = (x * jax.lax.rsqrt(ms + EPS) * ng_ref[...]).astype(jnp.bfloat16)
    cos = cos_ref[...]
    sin_signed = sin_ref[...]
    ones_bd = bd_ref[...]
    lane = jax.lax.broadcasted_iota(jnp.int32, (x.shape[0], LANES), 1)
    low = lane < HEAD_DIM

    scale = math.log2(math.e) / math.sqrt(HEAD_DIM)
    for j in range(ATTN_WIDTH // LANES):
        z = jnp.dot(h, w_ref[:, j * LANES:(j + 1) * LANES],
                    preferred_element_type=jnp.float32)
        q = _head_norm_rope(z, qg_ref[...], cos, sin_signed, ones_bd, scale)
        qt_ref[j * LANES:(j + 1) * LANES, :] = q.T.astype(qt_ref.dtype)

    z = jnp.dot(h, w_ref[:, ATTN_WIDTH:ATTN_WIDTH + KV_WIDTH],
                preferred_element_type=jnp.float32)
    k = _head_norm_rope(z, kg_ref[...], cos, sin_signed, ones_bd, 1.0)
    k_sw = pltpu.roll(k, HEAD_DIM, 1)
    zero = jnp.zeros_like(k)
    kx_ref[:, 0 * LANES:1 * LANES] = jnp.where(low, k, zero).astype(kx_ref.dtype)
    kx_ref[:, 1 * LANES:2 * LANES] = jnp.where(low, zero, k_sw).astype(kx_ref.dtype)
    kx_ref[:, 2 * LANES:3 * LANES] = jnp.where(low, k_sw, zero).astype(kx_ref.dtype)
    kx_ref[:, 3 * LANES:4 * LANES] = jnp.where(low, zero, k).astype(kx_ref.dtype)

    v = jnp.dot(h, w_ref[:, ATTN_WIDTH + KV_WIDTH:ATTN_WIDTH + 2 * KV_WIDTH],
                preferred_element_type=jnp.float32)
    v_sw = pltpu.roll(v, HEAD_DIM, 1)
    ones_col = jnp.where(lane == HEAD_DIM, 1.0, 0.0)
    vxt_ref[0 * LANES:1 * LANES, :] = jnp.where(low, v, ones_col).T.astype(vxt_ref.dtype)
    vxt_ref[1 * LANES:2 * LANES, :] = jnp.where(low, v_sw, ones_col).T.astype(vxt_ref.dtype)

    base = ATTN_WIDTH + 2 * KV_WIDTH
    step = 512
    for c in range(REST_WIDTH // step):
        z = jnp.dot(h, w_ref[:, base + c * step:base + (c + 1) * step],
                    preferred_element_type=jnp.float32)
        rest_ref[:, c * step:(c + 1) * step] = z.astype(rest_ref.dtype)


def _attn_kernel(qt_ref, kx_ref, vxt_ref, o_ref, m_ref, acc_ref, s0_ref, s1_ref, mc0_ref, mc1_ref):
    n_heads = GQA_GROUP
    seq = kx_ref.shape[0]
    n_chunks = seq // CK
    m_ref[...] = jnp.full(m_ref.shape, NEG_BIG, jnp.float32)
    acc_ref[...] = jnp.zeros(acc_ref.shape, jnp.float32)

    def scores(c, h, s_ref, mc_ref):
        start = pl.multiple_of(c * CK, CK)
        qt = qt_ref[(h // 2) * LANES:(h // 2 + 1) * LANES, :]
        kv = kx_ref[pl.ds(start, CK), (h % 2) * LANES:(h % 2 + 1) * LANES]
        s = jnp.dot(kv, qt, preferred_element_type=jnp.float32)
        s_ref[h] = s
        mc_ref[h] = jnp.broadcast_to(jnp.max(s, axis=0, keepdims=True), mc_ref.shape[1:])

    def accumulate(c, h, s_ref, mc_ref):
        start = pl.multiple_of(c * CK, CK)
        vxt = vxt_ref[0:V_ROWS, pl.ds(start, CK)]
        m_old = m_ref[h]
        m_new = jnp.maximum(m_old, mc_ref[h])
        alpha = jnp.exp2(m_old - m_new)
        p = jnp.exp2(s_ref[h] - m_new[0:1, :]).astype(jnp.bfloat16)
        acc_ref[h, 0:V_ROWS, :] = acc_ref[h, 0:V_ROWS, :] * alpha[0:1, :] + jnp.dot(
            vxt, p, preferred_element_type=jnp.float32)
        m_ref[h] = m_new

    def step(c, cur, nxt):
        for h in range(n_heads):
            scores(c + 1, h, *nxt)
            accumulate(c, h, *cur)

    buf0 = (s0_ref, mc0_ref)
    buf1 = (s1_ref, mc1_ref)
    for h in range(n_heads):
        scores(0, h, *buf0)

    bufs = (buf0, buf1)
    n_steps = n_chunks - 1
    n_loop = n_steps // STEPS_PER_BODY

    def body(i, carry):
        for u in range(STEPS_PER_BODY):
            step(STEPS_PER_BODY * i + u, bufs[u % 2], bufs[(u + 1) % 2])
        return carry

    jax.lax.fori_loop(0, n_loop, body, 0)
    for c in range(n_loop * STEPS_PER_BODY, n_steps):
        step(c, bufs[c % 2], bufs[(c + 1) % 2])
    for h in range(n_heads):
        accumulate(n_chunks - 1, h, *buf1)

    lane = jax.lax.broadcasted_iota(jnp.int32, (o_ref.shape[0], LANES), 1)
    for j in range(n_heads // 2):
        halves = []
        for h in (2 * j, 2 * j + 1):
            acc = acc_ref[h]
            halves.append((acc / acc[HEAD_DIM:HEAD_DIM + 1, :]).T)
        pair = jnp.where(lane < HEAD_DIM, halves[0], pltpu.roll(halves[1], HEAD_DIM, 1))
        o_ref[:, j * LANES:(j + 1) * LANES] = pair.astype(o_ref.dtype)


def _post_kernel(attn_ref, rest_ref, x_ref, wpa_ref, wpg_ref, wout_ref, ws_ref,
                 bsp_ref, lng_ref, lnb_ref, bm_ref, fg_ref, o_ref, vmix_ref):
    f32 = jnp.float32
    tm = x_ref.shape[0]

    def rest(lo, hi):
        return rest_ref[:, lo:hi].astype(f32)

    a = attn_ref[...] * jax.nn.silu(rest(0, ATTN_WIDTH))
    y_a = jnp.dot(a.astype(jnp.bfloat16), wpa_ref[...], preferred_element_type=f32)

    o0 = ATTN_WIDTH
    vg = jax.nn.gelu(rest(o0 + GMLP_WIDTH, o0 + 2 * GMLP_WIDTH))
    mu = jnp.mean(vg, axis=-1, keepdims=True)
    var = jnp.mean(jnp.square(vg - mu), axis=-1, keepdims=True)
    vv = ((vg - mu) * jax.lax.rsqrt(var + EPS) * lng_ref[...] + lnb_ref[...]).astype(jnp.bfloat16)
    group = jax.lax.broadcasted_iota(jnp.int32, (CHUNK, GMLP_WIDTH), 1) // GROUP_DIM
    for c in range(tm // CHUNK):
        r = jnp.dot(ws_ref[...], vv[c * CHUNK:(c + 1) * CHUNK, :], preferred_element_type=f32)
        mix = r[0:CHUNK, :]
        for g in range(1, N_GROUPS):
            mix = jnp.where(group == g, r[g * CHUNK:(g + 1) * CHUNK, :], mix)
        vmix_ref[c * CHUNK:(c + 1) * CHUNK, :] = mix + bsp_ref[...]
    u = jax.nn.gelu(rest(o0, o0 + GMLP_WIDTH))
    gm = u * vmix_ref[...] * jax.nn.silu(rest(o0 + 2 * GMLP_WIDTH, o0 + 3 * GMLP_WIDTH))
    y_b = jnp.dot(gm.astype(jnp.bfloat16), wpg_ref[...], preferred_element_type=f32)

    o1 = o0 + 3 * GMLP_WIDTH
    g0 = jax.nn.sigmoid(rest(o1, o1 + D_MODEL) + bm_ref[:, 0:D_MODEL])
    g1 = jax.nn.sigmoid(rest(o1 + D_MODEL, o1 + 2 * D_MODEL) + bm_ref[:, D_MODEL:2 * D_MODEL])
    y = g0 * y_a + g1 * y_b
    out = x_ref[...] + jnp.dot(y.astype(jnp.bfloat16), wout_ref[...], preferred_element_type=f32)
    ms = jnp.mean(out * out, axis=-1, keepdims=True)
    o_ref[...] = out * jax.lax.rsqrt(ms + EPS) * fg_ref[...]


def _rope_tables(seq):
    rows = seq // GRID_W
    row_idx = jnp.repeat(jnp.arange(rows, dtype=jnp.float32), GRID_W)
    col_idx = jnp.tile(jnp.arange(GRID_W, dtype=jnp.float32), rows)
    inv_freq = ROPE_THETA ** (-jnp.arange(0, AXIAL_DIM, 2, dtype=jnp.float32) / AXIAL_DIM)
    row_ang = row_idx[:, None] * inv_freq
    col_ang = col_idx[:, None] * inv_freq
    cos = jnp.concatenate([jnp.cos(row_ang)] * 2 + [jnp.cos(col_ang)] * 2, axis=-1)
    sin = jnp.concatenate([-jnp.sin(row_ang), jnp.sin(row_ang),
                           -jnp.sin(col_ang), jnp.sin(col_ang)], axis=-1)
    return jnp.tile(cos, (1, 2)), jnp.tile(sin, (1, 2))


def _layer(x2, batch, seq, norm_gain, w_in, q_gain, k_gain, w_proj_attn, ln_v_gain, ln_v_bias,
           w_spatial, b_spatial, w_proj_gmlp, b_merge, w_out, final_gain):
    n = batch * seq
    bf16 = jnp.bfloat16
    f32 = jnp.float32
    cos, sin = _rope_tables(seq)
    seg = np.arange(LANES) // HEAD_DIM
    ones_bd = jnp.asarray(seg[:, None] == seg[None, :], dtype=bf16)
    row = lambda v: v.reshape(1, -1).astype(f32)
    const = lambda shape: pl.BlockSpec(shape, lambda *_: (0,) * len(shape))
    in_width = w_in.shape[1]
    seq_blocks = seq // TM_IN

    qt, kx, vxt, rest = pl.pallas_call(
        _in_proj_kernel,
        grid=(n // TM_IN,),
        in_specs=[
            pl.BlockSpec((TM_IN, D_MODEL), lambda i: (i, 0)),
            const((1, D_MODEL)),
            const((D_MODEL, in_width)),
            const((1, LANES)),
            const((1, LANES)),
            pl.BlockSpec((TM_IN, LANES), lambda i: (i % seq_blocks, 0)),
            pl.BlockSpec((TM_IN, LANES), lambda i: (i % seq_blocks, 0)),
            const((LANES, LANES)),
        ],
        out_specs=[
            pl.BlockSpec((ATTN_WIDTH, TM_IN), lambda i: (0, i)),
            pl.BlockSpec((TM_IN, 4 * LANES), lambda i: (i, 0)),
            pl.BlockSpec((2 * LANES, TM_IN), lambda i: (0, i)),
            pl.BlockSpec((TM_IN, REST_WIDTH), lambda i: (i, 0)),
        ],
        out_shape=[
            jax.ShapeDtypeStruct((ATTN_WIDTH, n), bf16),
            jax.ShapeDtypeStruct((n, 4 * LANES), bf16),
            jax.ShapeDtypeStruct((2 * LANES, n), bf16),
            jax.ShapeDtypeStruct((n, REST_WIDTH), bf16),
        ],
        compiler_params=pltpu.CompilerParams(
            dimension_semantics=("arbitrary",), vmem_limit_bytes=VMEM_LIMIT_BYTES),
        name="in_proj",
    )(x2, row(norm_gain), w_in.astype(bf16), row(jnp.tile(q_gain, 2)), row(jnp.tile(k_gain, 2)),
      cos, sin, ones_bd)

    nq = seq // TQ
    kv_width = GQA_GROUP * HEAD_DIM
    attn = pl.pallas_call(
        _attn_kernel,
        grid=(batch, N_KV_HEADS, nq),
        in_specs=[
            pl.BlockSpec((kv_width, TQ), lambda b, g, i: (g, b * nq + i)),
            pl.BlockSpec((seq, 2 * LANES), lambda b, g, i: (b, g)),
            pl.BlockSpec((LANES, seq), lambda b, g, i: (g, b)),
        ],
        out_specs=pl.BlockSpec((TQ, kv_width), lambda b, g, i: (b * nq + i, g)),
        out_shape=jax.ShapeDtypeStruct((n, ATTN_WIDTH), f32),
        scratch_shapes=[
            pltpu.VMEM((GQA_GROUP, 8, TQ), f32),
            pltpu.VMEM((GQA_GROUP, LANES, TQ), f32),
            pltpu.VMEM((GQA_GROUP, CK, TQ), f32),
            pltpu.VMEM((GQA_GROUP, CK, TQ), f32),
            pltpu.VMEM((GQA_GROUP, 8, TQ), f32),
            pltpu.VMEM((GQA_GROUP, 8, TQ), f32),
        ],
        compiler_params=pltpu.CompilerParams(
            dimension_semantics=("arbitrary", "arbitrary", "arbitrary"),
            vmem_limit_bytes=VMEM_LIMIT_BYTES),
        name="attention",
    )(qt, kx, vxt)

    ws_stack = w_spatial.reshape(N_GROUPS * CHUNK, CHUNK).astype(bf16)
    b_sp = jnp.repeat(b_spatial.T, GROUP_DIM, axis=1).astype(f32)
    out = pl.pallas_call(
        _post_kernel,
        grid=(n // TM_POST,),
        in_specs=[
            pl.BlockSpec((TM_POST, ATTN_WIDTH), lambda i: (i, 0)),
            pl.BlockSpec((TM_POST, REST_WIDTH), lambda i: (i, 0)),
            pl.BlockSpec((TM_POST, D_MODEL), lambda i: (i, 0)),
            const((ATTN_WIDTH, D_MODEL)),
            const((GMLP_WIDTH, D_MODEL)),
            const((D_MODEL, D_MODEL)),
            const((N_GROUPS * CHUNK, CHUNK)),
            const((CHUNK, GMLP_WIDTH)),
            const((1, GMLP_WIDTH)),
            const((1, GMLP_WIDTH)),
            const((1, N_BRANCHES * D_MODEL)),
            const((1, D_MODEL)),
        ],
        out_specs=pl.BlockSpec((TM_POST, D_MODEL), lambda i: (i, 0)),
        out_shape=jax.ShapeDtypeStruct((n, D_MODEL), f32),
        scratch_shapes=[pltpu.VMEM((TM_POST, GMLP_WIDTH), f32)],
        compiler_params=pltpu.CompilerParams(
            dimension_semantics=("arbitrary",), vmem_limit_bytes=VMEM_LIMIT_BYTES),
        name="post",
    )(attn, rest, x2, w_proj_attn.astype(bf16), w_proj_gmlp.astype(bf16), w_out.astype(bf16),
      ws_stack, b_sp, row(ln_v_gain), row(ln_v_bias), row(b_merge), row(final_gain))
    return out


def kernel(x, norm_gain, w_in, q_gain, k_gain, w_proj_attn, ln_v_gain, ln_v_bias, w_spatial,
           b_spatial, w_proj_gmlp, b_merge, w_out, final_gain):
    batch, seq, d = x.shape
    depth = w_in.shape[0]
    assert depth == 1, "the final rmsnorm is fused into the single layer's last call"
    assert d == D_MODEL and seq % TM_IN == 0 and seq % TQ == 0 and seq % CK == 0
    out = _layer(x.reshape(batch * seq, d), batch, seq, norm_gain[0], w_in[0], q_gain[0],
                 k_gain[0], w_proj_attn[0], ln_v_gain[0], ln_v_bias[0], w_spatial[0],
                 b_spatial[0], w_proj_gmlp[0], b_merge[0], w_out[0], final_gain)
    return out.reshape(batch, seq, d)
```

```python
import functools
import math

import jax
import jax.numpy as jnp
import numpy as np
from jax.experimental import pallas as pl
from jax.experimental.pallas import tpu as pltpu

D_MODEL = 1024
GRID_W = 64
HEAD_DIM = 64
N_Q_HEADS = 8
N_KV_HEADS = 2
GQA_GROUP = N_Q_HEADS // N_KV_HEADS
ATTN_WIDTH = N_Q_HEADS * HEAD_DIM
KV_WIDTH = N_KV_HEADS * HEAD_DIM
AXIAL_DIM = HEAD_DIM // 2
ROPE_THETA = 10000.0
CHUNK = 128
N_GROUPS = 8
GMLP_WIDTH = 512
GROUP_DIM = GMLP_WIDTH // N_GROUPS
N_BRANCHES = 2
EPS = 1e-6
REST_WIDTH = ATTN_WIDTH + 3 * GMLP_WIDTH + N_BRANCHES * D_MODEL

LANES = 128
VMEM_LIMIT_BYTES = 56 * 1024 * 1024

TM_IN = 512
REST_STEP = 512
TQ = 256
CK = 512
STEPS_PER_BODY = 6
V_ROWS = 80
TM_POST = 512
NEG_BIG = -1e30


def _segment_sums(z, bd_stack):
    sq = z * z
    hi = sq.astype(jnp.bfloat16)
    lo = (sq - hi.astype(jnp.float32)).astype(jnp.bfloat16)
    return jnp.dot(jnp.concatenate([hi, lo], axis=1), bd_stack,
                   preferred_element_type=jnp.float32)


def _head_norm_rope(z, sums, gain, cos, sin_signed, scale):
    ms = sums * (1.0 / HEAD_DIM)
    y = z * (jax.lax.rsqrt(ms + EPS) * scale) * gain
    lane = jax.lax.broadcasted_iota(jnp.int32, y.shape, 1)
    half = AXIAL_DIM // 2
    partner = jnp.where((lane & half) != 0,
                        pltpu.roll(y, half, 1),
                        pltpu.roll(y, LANES - half, 1))
    return y * cos + partner * sin_signed


def _in_proj_kernel(x_ref, ng_ref, w_ref, qg_ref, kg_ref, cos_ref, sin_ref, bd_ref,
                    lng_ref, lnb_ref, bm_ref, qt_ref, kx_ref, vxt_ref, rest_ref):
    f32 = jnp.float32
    x = x_ref[...]
    ms = jnp.mean(x * x, axis=-1, keepdims=True)
    h = (x * jax.lax.rsqrt(ms + EPS) * ng_ref[...]).astype(jnp.bfloat16)

    def layernorm_v(z):
        g = jax.nn.gelu(z)
        mu = jnp.mean(g, axis=-1, keepdims=True)
        var = jnp.mean(jnp.square(g - mu), axis=-1, keepdims=True)
        return (g - mu) * jax.lax.rsqrt(var + EPS) * lng_ref[...] + lnb_ref[...]

    def merge_gate(c):
        lo = (c - 4) * REST_STEP
        return lambda z: jax.nn.sigmoid(z + bm_ref[:, lo:lo + REST_STEP])

    activations = [jax.nn.silu, jax.nn.gelu, layernorm_v, jax.nn.silu] + [
        merge_gate(c) for c in range(4, REST_WIDTH // REST_STEP)]

    def rest_chunks(lo, hi):
        base = ATTN_WIDTH + 2 * KV_WIDTH
        for c in range(lo, hi):
            z = jnp.dot(h, w_ref[:, base + c * REST_STEP:base + (c + 1) * REST_STEP],
                        preferred_element_type=f32)
            rest_ref[:, c * REST_STEP:(c + 1) * REST_STEP] = activations[c](z).astype(rest_ref.dtype)

    n_rest = REST_WIDTH // REST_STEP
    zq = jnp.dot(h, w_ref[:, 0:ATTN_WIDTH], preferred_element_type=f32)
    zkv = jnp.dot(h, w_ref[:, ATTN_WIDTH:ATTN_WIDTH + 2 * KV_WIDTH], preferred_element_type=f32)
    rest_chunks(0, n_rest // 2)
    bd_stack = bd_ref[...]
    sums_q = [_segment_sums(zq[:, i * 2 * LANES:(i + 1) * 2 * LANES], bd_stack)
              for i in range(ATTN_WIDTH // (2 * LANES))]
    sums_kv = _segment_sums(zkv, bd_stack)
    rest_chunks(n_rest // 2, n_rest)

    cos = cos_ref[...]
    sin_signed = sin_ref[...]
    lane = jax.lax.broadcasted_iota(jnp.int32, (x.shape[0], LANES), 1)
    low = lane < HEAD_DIM

    scale = math.log2(math.e) / math.sqrt(HEAD_DIM)
    for j in range(ATTN_WIDTH // LANES):
        sums = sums_q[j // 2][:, (j % 2) * LANES:(j % 2 + 1) * LANES]
        q = _head_norm_rope(zq[:, j * LANES:(j + 1) * LANES], sums, qg_ref[...],
                            cos, sin_signed, scale)
        qt_ref[j * LANES:(j + 1) * LANES, :] = q.T.astype(qt_ref.dtype)

    k = _head_norm_rope(zkv[:, 0:LANES], sums_kv[:, 0:LANES], kg_ref[...], cos, sin_signed, 1.0)
    k_sw = pltpu.roll(k, HEAD_DIM, 1)
    zero = jnp.zeros_like(k)
    kx_ref[:, 0 * LANES:1 * LANES] = jnp.where(low, k, zero).astype(kx_ref.dtype)
    kx_ref[:, 1 * LANES:2 * LANES] = jnp.where(low, zero, k_sw).astype(kx_ref.dtype)
    kx_ref[:, 2 * LANES:3 * LANES] = jnp.where(low, k_sw, zero).astype(kx_ref.dtype)
    kx_ref[:, 3 * LANES:4 * LANES] = jnp.where(low, zero, k).astype(kx_ref.dtype)

    v = zkv[:, LANES:2 * LANES]
    v_sw = pltpu.roll(v, HEAD_DIM, 1)
    ones_col = jnp.where(lane == HEAD_DIM, 1.0, 0.0)
    vxt_ref[0 * LANES:1 * LANES, :] = jnp.where(low, v, ones_col).T.astype(vxt_ref.dtype)
    vxt_ref[1 * LANES:2 * LANES, :] = jnp.where(low, v_sw, ones_col).T.astype(vxt_ref.dtype)


def _attn_kernel(qt_ref, kx_ref, vxt_ref, o_ref, m_ref, acc_ref, s0_ref, s1_ref, mc0_ref, mc1_ref):
    n_heads = GQA_GROUP
    seq = kx_ref.shape[0]
    n_chunks = seq // CK
    m_ref[...] = jnp.full(m_ref.shape, NEG_BIG, jnp.float32)
    acc_ref[...] = jnp.zeros(acc_ref.shape, jnp.float32)

    def scores(c, h, s_ref, mc_ref):
        start = pl.multiple_of(c * CK, CK)
        qt = qt_ref[(h // 2) * LANES:(h // 2 + 1) * LANES, :]
        kv = kx_ref[pl.ds(start, CK), (h % 2) * LANES:(h % 2 + 1) * LANES]
        s = jnp.dot(kv, qt, preferred_element_type=jnp.float32)
        s_ref[h] = s
        mc_ref[h] = jnp.broadcast_to(jnp.max(s, axis=0, keepdims=True), mc_ref.shape[1:])

    def accumulate(c, h, s_ref, mc_ref):
        start = pl.multiple_of(c * CK, CK)
        vxt = vxt_ref[0:V_ROWS, pl.ds(start, CK)]
        m_old = m_ref[h]
        m_new = jnp.maximum(m_old, mc_ref[h])
        alpha = jnp.exp2(m_old - m_new)
        p = jnp.exp2(s_ref[h] - m_new[0:1, :]).astype(jnp.bfloat16)
        acc_ref[h, 0:V_ROWS, :] = acc_ref[h, 0:V_ROWS, :] * alpha[0:1, :] + jnp.dot(
            vxt, p, preferred_element_type=jnp.float32)
        m_ref[h] = m_new

    def step(c, cur, nxt):
        for h in range(n_heads):
            scores(c + 1, h, *nxt)
            accumulate(c, h, *cur)

    buf0 = (s0_ref, mc0_ref)
    buf1 = (s1_ref, mc1_ref)
    for h in range(n_heads):
        scores(0, h, *buf0)

    bufs = (buf0, buf1)
    n_steps = n_chunks - 1
    n_loop = n_steps // STEPS_PER_BODY

    def body(i, carry):
        for u in range(STEPS_PER_BODY):
            step(STEPS_PER_BODY * i + u, bufs[u % 2], bufs[(u + 1) % 2])
        return carry

    jax.lax.fori_loop(0, n_loop, body, 0)
    for c in range(n_loop * STEPS_PER_BODY, n_steps):
        step(c, bufs[c % 2], bufs[(c + 1) % 2])
    for h in range(n_heads):
        accumulate(n_chunks - 1, h, *buf1)

    lane = jax.lax.broadcasted_iota(jnp.int32, (o_ref.shape[0], LANES), 1)
    for j in range(n_heads // 2):
        halves = []
        for h in (2 * j, 2 * j + 1):
            acc = acc_ref[h]
            halves.append((acc / acc[HEAD_DIM:HEAD_DIM + 1, :]).T)
        pair = jnp.where(lane < HEAD_DIM, halves[0], pltpu.roll(halves[1], HEAD_DIM, 1))
        o_ref[:, j * LANES:(j + 1) * LANES] = pair.astype(o_ref.dtype)


def _post_kernel(attn_ref, rest_ref, x_ref, wpa_ref, wpg_ref, wout_ref, ws_ref,
                 bsp_ref, fg_ref, o_ref, vmix_ref):
    f32 = jnp.float32
    tm = x_ref.shape[0]

    def rest(lo, hi):
        return rest_ref[:, lo:hi].astype(f32)

    a = attn_ref[...] * rest(0, ATTN_WIDTH)
    y_a = jnp.dot(a.astype(jnp.bfloat16), wpa_ref[...], preferred_element_type=f32)

    o0 = ATTN_WIDTH
    vv = rest_ref[:, o0 + GMLP_WIDTH:o0 + 2 * GMLP_WIDTH]
    low = jax.lax.broadcasted_iota(jnp.int32, (CHUNK, LANES), 1) < GROUP_DIM
    zero = jnp.zeros((CHUNK, LANES), jnp.bfloat16)
    for c in range(tm // CHUNK):
        slabs = []
        for j in range(GMLP_WIDTH // LANES):
            v_slab = vv[c * CHUNK:(c + 1) * CHUNK, j * LANES:(j + 1) * LANES]
            rhs = jnp.concatenate([jnp.where(low, v_slab, zero), jnp.where(low, zero, v_slab)],
                                  axis=0)
            slabs.append(jnp.dot(ws_ref[j], rhs, preferred_element_type=f32))
        vmix_ref[c * CHUNK:(c + 1) * CHUNK, :] = jnp.concatenate(slabs, axis=1) + bsp_ref[...]
    gm = rest(o0, o0 + GMLP_WIDTH) * vmix_ref[...] * rest(o0 + 2 * GMLP_WIDTH, o0 + 3 * GMLP_WIDTH)
    y_b = jnp.dot(gm.astype(jnp.bfloat16), wpg_ref[...], preferred_element_type=f32)

    o1 = o0 + 3 * GMLP_WIDTH
    y = rest(o1, o1 + D_MODEL) * y_a + rest(o1 + D_MODEL, o1 + 2 * D_MODEL) * y_b
    out = x_ref[...] + jnp.dot(y.astype(jnp.bfloat16), wout_ref[...], preferred_element_type=f32)
    ms = jnp.mean(out * out, axis=-1, keepdims=True)
    o_ref[...] = out * jax.lax.rsqrt(ms + EPS) * fg_ref[...]


def _rope_tables(seq):
    rows = seq // GRID_W
    row_idx = jnp.repeat(jnp.arange(rows, dtype=jnp.float32), GRID_W)
    col_idx = jnp.tile(jnp.arange(GRID_W, dtype=jnp.float32), rows)
    inv_freq = ROPE_THETA ** (-jnp.arange(0, AXIAL_DIM, 2, dtype=jnp.float32) / AXIAL_DIM)
    row_ang = row_idx[:, None] * inv_freq
    col_ang = col_idx[:, None] * inv_freq
    cos = jnp.concatenate([jnp.cos(row_ang)] * 2 + [jnp.cos(col_ang)] * 2, axis=-1)
    sin = jnp.concatenate([-jnp.sin(row_ang), jnp.sin(row_ang),
                           -jnp.sin(col_ang), jnp.sin(col_ang)], axis=-1)
    return jnp.tile(cos, (1, 2)), jnp.tile(sin, (1, 2))


def _layer(x2, batch, seq, norm_gain, w_in, q_gain, k_gain, w_proj_attn, ln_v_gain, ln_v_bias,
           w_spatial, b_spatial, w_proj_gmlp, b_merge, w_out, final_gain):
    n = batch * seq
    bf16 = jnp.bfloat16
    f32 = jnp.float32
    cos, sin = _rope_tables(seq)
    seg = np.arange(2 * LANES) // HEAD_DIM
    block_diag = seg[:, None] == seg[None, :]
    ones_bd = jnp.asarray(np.concatenate([block_diag, block_diag], axis=0), dtype=bf16)
    row = lambda v: v.reshape(1, -1).astype(f32)
    const = lambda shape: pl.BlockSpec(shape, lambda *_: (0,) * len(shape))
    in_width = w_in.shape[1]
    seq_blocks = seq // TM_IN

    qt, kx, vxt, rest = pl.pallas_call(
        _in_proj_kernel,
        grid=(n // TM_IN,),
        in_specs=[
            pl.BlockSpec((TM_IN, D_MODEL), lambda i: (i, 0)),
            const((1, D_MODEL)),
            const((D_MODEL, in_width)),
            const((1, LANES)),
            const((1, LANES)),
            pl.BlockSpec((TM_IN, LANES), lambda i: (i % seq_blocks, 0)),
            pl.BlockSpec((TM_IN, LANES), lambda i: (i % seq_blocks, 0)),
            const((4 * LANES, 2 * LANES)),
            const((1, GMLP_WIDTH)),
            const((1, GMLP_WIDTH)),
            const((1, N_BRANCHES * D_MODEL)),
        ],
        out_specs=[
            pl.BlockSpec((ATTN_WIDTH, TM_IN), lambda i: (0, i)),
            pl.BlockSpec((TM_IN, 4 * LANES), lambda i: (i, 0)),
            pl.BlockSpec((2 * LANES, TM_IN), lambda i: (0, i)),
            pl.BlockSpec((TM_IN, REST_WIDTH), lambda i: (i, 0)),
        ],
        out_shape=[
            jax.ShapeDtypeStruct((ATTN_WIDTH, n), bf16),
            jax.ShapeDtypeStruct((n, 4 * LANES), bf16),
            jax.ShapeDtypeStruct((2 * LANES, n), bf16),
            jax.ShapeDtypeStruct((n, REST_WIDTH), bf16),
        ],
        compiler_params=pltpu.CompilerParams(
            dimension_semantics=("arbitrary",), vmem_limit_bytes=VMEM_LIMIT_BYTES),
        name="in_proj",
    )(x2, row(norm_gain), w_in.astype(bf16), row(jnp.tile(q_gain, 2)), row(jnp.tile(k_gain, 2)),
      cos, sin, ones_bd, row(ln_v_gain), row(ln_v_bias), row(b_merge))

    nq = seq // TQ
    kv_width = GQA_GROUP * HEAD_DIM
    attn = pl.pallas_call(
        _attn_kernel,
        grid=(batch, N_KV_HEADS, nq),
        in_specs=[
            pl.BlockSpec((kv_width, TQ), lambda b, g, i: (g, b * nq + i)),
            pl.BlockSpec((seq, 2 * LANES), lambda b, g, i: (b, g)),
            pl.BlockSpec((LANES, seq), lambda b, g, i: (g, b)),
        ],
        out_specs=pl.BlockSpec((TQ, kv_width), lambda b, g, i: (b * nq + i, g)),
        out_shape=jax.ShapeDtypeStruct((n, ATTN_WIDTH), f32),
        scratch_shapes=[
            pltpu.VMEM((GQA_GROUP, 8, TQ), f32),
            pltpu.VMEM((GQA_GROUP, LANES, TQ), f32),
            pltpu.VMEM((GQA_GROUP, CK, TQ), f32),
            pltpu.VMEM((GQA_GROUP, CK, TQ), f32),
            pltpu.VMEM((GQA_GROUP, 8, TQ), f32),
            pltpu.VMEM((GQA_GROUP, 8, TQ), f32),
        ],
        compiler_params=pltpu.CompilerParams(
            dimension_semantics=("arbitrary", "arbitrary", "arbitrary"),
            vmem_limit_bytes=VMEM_LIMIT_BYTES),
        name="attention",
    )(qt, kx, vxt)

    ws_stack = w_spatial.reshape(N_GROUPS // 2, 2, CHUNK, CHUNK).transpose(0, 2, 1, 3).reshape(
        N_GROUPS // 2, CHUNK, 2 * CHUNK).astype(bf16)
    b_sp = jnp.repeat(b_spatial.T, GROUP_DIM, axis=1).astype(f32)
    out = pl.pallas_call(
        _post_kernel,
        grid=(n // TM_POST,),
        in_specs=[
            pl.BlockSpec((TM_POST, ATTN_WIDTH), lambda i: (i, 0)),
            pl.BlockSpec((TM_POST, REST_WIDTH), lambda i: (i, 0)),
            pl.BlockSpec((TM_POST, D_MODEL), lambda i: (i, 0)),
            const((ATTN_WIDTH, D_MODEL)),
            const((GMLP_WIDTH, D_MODEL)),
            const((D_MODEL, D_MODEL)),
            const((N_GROUPS // 2, CHUNK, 2 * CHUNK)),
            const((CHUNK, GMLP_WIDTH)),
            const((1, D_MODEL)),
        ],
        out_specs=pl.BlockSpec((TM_POST, D_MODEL), lambda i: (i, 0)),
        out_shape=jax.ShapeDtypeStruct((n, D_MODEL), f32),
        scratch_shapes=[pltpu.VMEM((TM_POST, GMLP_WIDTH), f32)],
        compiler_params=pltpu.CompilerParams(
            dimension_semantics=("arbitrary",), vmem_limit_bytes=VMEM_LIMIT_BYTES),
        name="post",
    )(attn, rest, x2, w_proj_attn.astype(bf16), w_proj_gmlp.astype(bf16), w_out.astype(bf16),
      ws_stack, b_sp, row(final_gain))
    return out


def kernel(x, norm_gain, w_in, q_gain, k_gain, w_proj_attn, ln_v_gain, ln_v_bias, w_spatial,
           b_spatial, w_proj_gmlp, b_merge, w_out, final_gain):
    batch, seq, d = x.shape
    depth = w_in.shape[0]
    assert depth == 1, "the final rmsnorm is fused into the single layer's last call"
    assert d == D_MODEL and seq % TM_IN == 0 and seq % TQ == 0 and seq % CK == 0
    out = _layer(x.reshape(batch * seq, d), batch, seq, norm_gain[0], w_in[0], q_gain[0],
                 k_gain[0], w_proj_attn[0], ln_v_gain[0], ln_v_bias[0], w_spatial[0],
                 b_spatial[0], w_proj_gmlp[0], b_merge[0], w_out[0], final_gain)
    return out.reshape(batch, seq, d)
```

```python
import functools
import math

import jax
import jax.numpy as jnp
import numpy as np
from jax.experimental import pallas as pl
from jax.experimental.pallas import tpu as pltpu

D_MODEL = 1024
GRID_W = 64
HEAD_DIM = 64
N_Q_HEADS = 8
N_KV_HEADS = 2
GQA_GROUP = N_Q_HEADS // N_KV_HEADS
ATTN_WIDTH = N_Q_HEADS * HEAD_DIM
KV_WIDTH = N_KV_HEADS * HEAD_DIM
AXIAL_DIM = HEAD_DIM // 2
ROPE_THETA = 10000.0
CHUNK = 128
N_GROUPS = 8
GMLP_WIDTH = 512
GROUP_DIM = GMLP_WIDTH // N_GROUPS
N_BRANCHES = 2
EPS = 1e-6
REST_WIDTH = ATTN_WIDTH + 3 * GMLP_WIDTH + N_BRANCHES * D_MODEL

LANES = 128
VMEM_LIMIT_BYTES = 56 * 1024 * 1024

TM_IN = 512
REST_STEP = 512
TQ = 256
CK = 512
STEPS_PER_BODY = 6
V_ROWS = 80
TM_POST = 512
NEG_BIG = -1e30


def _segment_sums(z, bd_stack):
    sq = z * z
    hi = sq.astype(jnp.bfloat16)
    lo = (sq - hi.astype(jnp.float32)).astype(jnp.bfloat16)
    return jnp.dot(jnp.concatenate([hi, lo], axis=1), bd_stack,
                   preferred_element_type=jnp.float32)


def _head_norm_rope(z, sums, gain, cos, sin_signed, scale):
    ms = sums * (1.0 / HEAD_DIM)
    y = z * (jax.lax.rsqrt(ms + EPS) * scale) * gain
    lane = jax.lax.broadcasted_iota(jnp.int32, y.shape, 1)
    half = AXIAL_DIM // 2
    partner = jnp.where((lane & half) != 0,
                        pltpu.roll(y, half, 1),
                        pltpu.roll(y, LANES - half, 1))
    return y * cos + partner * sin_signed


def _in_proj_kernel(x_ref, ng_ref, w_ref, qg_ref, kg_ref, cos_ref, sin_ref, bd_ref,
                    lng_ref, lnb_ref, bm_ref, qt_ref, kx_ref, vxt_ref, rest_ref):
    f32 = jnp.float32
    x = x_ref[...]
    ms = jnp.mean(x * x, axis=-1, keepdims=True)
    h = (x * jax.lax.rsqrt(ms + EPS) * ng_ref[...]).astype(jnp.bfloat16)

    def layernorm_v(z):
        g = jax.nn.gelu(z)
        mu = jnp.mean(g, axis=-1, keepdims=True)
        var = jnp.mean(jnp.square(g - mu), axis=-1, keepdims=True)
        return (g - mu) * jax.lax.rsqrt(var + EPS) * lng_ref[...] + lnb_ref[...]

    def merge_gate(c):
        lo = (c - 4) * REST_STEP
        return lambda z: jax.nn.sigmoid(z + bm_ref[:, lo:lo + REST_STEP])

    activations = [jax.nn.silu, jax.nn.gelu, layernorm_v, jax.nn.silu] + [
        merge_gate(c) for c in range(4, REST_WIDTH // REST_STEP)]

    def rest_chunks(lo, hi):
        base = ATTN_WIDTH + 2 * KV_WIDTH
        for c in range(lo, hi):
            z = jnp.dot(h, w_ref[:, base + c * REST_STEP:base + (c + 1) * REST_STEP],
                        preferred_element_type=f32)
            rest_ref[:, c * REST_STEP:(c + 1) * REST_STEP] = activations[c](z).astype(rest_ref.dtype)

    n_rest = REST_WIDTH // REST_STEP
    zq = jnp.dot(h, w_ref[:, 0:ATTN_WIDTH], preferred_element_type=f32)
    zkv = jnp.dot(h, w_ref[:, ATTN_WIDTH:ATTN_WIDTH + 2 * KV_WIDTH], preferred_element_type=f32)
    rest_chunks(0, n_rest // 2)
    bd_stack = bd_ref[...]
    sums_q = [_segment_sums(zq[:, i * 2 * LANES:(i + 1) * 2 * LANES], bd_stack)
              for i in range(ATTN_WIDTH // (2 * LANES))]
    sums_kv = _segment_sums(zkv, bd_stack)
    rest_chunks(n_rest // 2, n_rest)

    cos = cos_ref[...]
    sin_signed = sin_ref[...]
    lane = jax.lax.broadcasted_iota(jnp.int32, (x.shape[0], LANES), 1)
    low = lane < HEAD_DIM

    scale = math.log2(math.e) / math.sqrt(HEAD_DIM)
    for j in range(ATTN_WIDTH // LANES):
        sums = sums_q[j // 2][:, (j % 2) * LANES:(j % 2 + 1) * LANES]
        q = _head_norm_rope(zq[:, j * LANES:(j + 1) * LANES], sums, qg_ref[...],
                            cos, sin_signed, scale)
        qt_ref[j * LANES:(j + 1) * LANES, :] = q.T.astype(qt_ref.dtype)

    k = _head_norm_rope(zkv[:, 0:LANES], sums_kv[:, 0:LANES], kg_ref[...], cos, sin_signed, 1.0)
    k_sw = pltpu.roll(k, HEAD_DIM, 1)
    zero = jnp.zeros_like(k)
    kx_ref[:, 0 * LANES:1 * LANES] = jnp.where(low, k, zero).astype(kx_ref.dtype)
    kx_ref[:, 1 * LANES:2 * LANES] = jnp.where(low, zero, k_sw).astype(kx_ref.dtype)
    kx_ref[:, 2 * LANES:3 * LANES] = jnp.where(low, k_sw, zero).astype(kx_ref.dtype)
    kx_ref[:, 3 * LANES:4 * LANES] = jnp.where(low, zero, k).astype(kx_ref.dtype)

    v = zkv[:, LANES:2 * LANES]
    v_sw = pltpu.roll(v, HEAD_DIM, 1)
    ones_col = jnp.where(lane == HEAD_DIM, 1.0, 0.0)
    vxt_ref[0 * LANES:1 * LANES, :] = jnp.where(low, v, ones_col).T.astype(vxt_ref.dtype)
    vxt_ref[1 * LANES:2 * LANES, :] = jnp.where(low, v_sw, ones_col).T.astype(vxt_ref.dtype)


def _attn_kernel(qt_ref, kx_ref, vxt_ref, o_ref, m_ref, acc_ref, s0_ref, s1_ref, mc0_ref, mc1_ref):
    n_heads = GQA_GROUP
    seq = kx_ref.shape[0]
    n_chunks = seq // CK
    n_qblocks = seq // TQ
    bufs = ((s0_ref, mc0_ref), (s1_ref, mc1_ref))
    assert n_chunks % 2 == 0 and STEPS_PER_BODY % 2 == 0

    def offset(i, size):
        return i * size if isinstance(i, int) else pl.multiple_of(i * size, size)

    def scores(qb, c, h, s_ref, mc_ref):
        qt = qt_ref[(h // 2) * LANES:(h // 2 + 1) * LANES, pl.ds(offset(qb, TQ), TQ)]
        kv = kx_ref[pl.ds(offset(c, CK), CK), (h % 2) * LANES:(h % 2 + 1) * LANES]
        s = jnp.dot(kv, qt, preferred_element_type=jnp.float32)
        s_ref[h] = s
        mc_ref[h] = jnp.broadcast_to(jnp.max(s, axis=0, keepdims=True), mc_ref.shape[1:])

    def accumulate(c, h, s_ref, mc_ref):
        vxt = vxt_ref[0:V_ROWS, pl.ds(offset(c, CK), CK)]
        m_old = m_ref[h]
        m_new = jnp.maximum(m_old, mc_ref[h])
        alpha = jnp.exp2(m_old - m_new)
        p = jnp.exp2(s_ref[h] - m_new[0:1, :]).astype(jnp.bfloat16)
        acc_ref[h, 0:V_ROWS, :] = acc_ref[h, 0:V_ROWS, :] * alpha[0:1, :] + jnp.dot(
            vxt, p, preferred_element_type=jnp.float32)
        m_ref[h] = m_new

    def step(qb, c, parity, qb_next, c_next):
        for h in range(n_heads):
            scores(qb_next, c_next, h, *bufs[1 - parity])
            accumulate(c, h, *bufs[parity])

    def finalize(qb):
        lane = jax.lax.broadcasted_iota(jnp.int32, (TQ, LANES), 1)
        for j in range(n_heads // 2):
            halves = []
            for h in (2 * j, 2 * j + 1):
                acc = acc_ref[h]
                halves.append((acc * (1.0 / acc[HEAD_DIM:HEAD_DIM + 1, :])).T)
            pair = jnp.where(lane < HEAD_DIM, halves[0], pltpu.roll(halves[1], HEAD_DIM, 1))
            o_ref[pl.ds(offset(qb, TQ), TQ), j * LANES:(j + 1) * LANES] = pair.astype(o_ref.dtype)
        m_ref[...] = jnp.full(m_ref.shape, NEG_BIG, jnp.float32)

    m_ref[...] = jnp.full(m_ref.shape, NEG_BIG, jnp.float32)
    acc_ref[...] = jnp.zeros(acc_ref.shape, jnp.float32)
    for h in range(n_heads):
        scores(0, 0, h, *bufs[0])

    n_loop = (n_chunks - 1) // STEPS_PER_BODY

    def qblock(qb, carry):
        def body(i, carry):
            for u in range(STEPS_PER_BODY):
                c = STEPS_PER_BODY * i + u
                step(qb, c, u % 2, qb, c + 1)
            return carry

        jax.lax.fori_loop(0, n_loop, body, 0)
        for c in range(n_loop * STEPS_PER_BODY, n_chunks - 1):
            step(qb, c, c % 2, qb, c + 1)
        step(qb, n_chunks - 1, 1, jnp.minimum(qb + 1, n_qblocks - 1), 0)
        finalize(qb)
        return carry

    jax.lax.fori_loop(0, n_qblocks, qblock, 0)


def _post_kernel(attn_ref, rest_ref, x_ref, wpa_ref, wpg_ref, wout_ref, ws_ref,
                 bsp_ref, fg_ref, o_ref, vmix_ref):
    f32 = jnp.float32
    tm = x_ref.shape[0]

    def rest(lo, hi):
        return rest_ref[:, lo:hi].astype(f32)

    a = attn_ref[...] * rest(0, ATTN_WIDTH)
    y_a = jnp.dot(a.astype(jnp.bfloat16), wpa_ref[...], preferred_element_type=f32)

    o0 = ATTN_WIDTH
    vv = rest_ref[:, o0 + GMLP_WIDTH:o0 + 2 * GMLP_WIDTH]
    low = jax.lax.broadcasted_iota(jnp.int32, (CHUNK, LANES), 1) < GROUP_DIM
    zero = jnp.zeros((CHUNK, LANES), jnp.bfloat16)
    for c in range(tm // CHUNK):
        slabs = []
        for j in range(GMLP_WIDTH // LANES):
            v_slab = vv[c * CHUNK:(c + 1) * CHUNK, j * LANES:(j + 1) * LANES]
            rhs = jnp.concatenate([jnp.where(low, v_slab, zero), jnp.where(low, zero, v_slab)],
                                  axis=0)
            slabs.append(jnp.dot(ws_ref[j], rhs, preferred_element_type=f32))
        vmix_ref[c * CHUNK:(c + 1) * CHUNK, :] = jnp.concatenate(slabs, axis=1) + bsp_ref[...]
    gm = rest(o0, o0 + GMLP_WIDTH) * vmix_ref[...] * rest(o0 + 2 * GMLP_WIDTH, o0 + 3 * GMLP_WIDTH)
    y_b = jnp.dot(gm.astype(jnp.bfloat16), wpg_ref[...], preferred_element_type=f32)

    o1 = o0 + 3 * GMLP_WIDTH
    y = rest(o1, o1 + D_MODEL) * y_a + rest(o1 + D_MODEL, o1 + 2 * D_MODEL) * y_b
    out = x_ref[...] + jnp.dot(y.astype(jnp.bfloat16), wout_ref[...], preferred_element_type=f32)
    ms = jnp.mean(out * out, axis=-1, keepdims=True)
    o_ref[...] = out * jax.lax.rsqrt(ms + EPS) * fg_ref[...]


def _rope_tables(seq):
    rows = seq // GRID_W
    row_idx = jnp.repeat(jnp.arange(rows, dtype=jnp.float32), GRID_W)
    col_idx = jnp.tile(jnp.arange(GRID_W, dtype=jnp.float32), rows)
    inv_freq = ROPE_THETA ** (-jnp.arange(0, AXIAL_DIM, 2, dtype=jnp.float32) / AXIAL_DIM)
    row_ang = row_idx[:, None] * inv_freq
    col_ang = col_idx[:, None] * inv_freq
    cos = jnp.concatenate([jnp.cos(row_ang)] * 2 + [jnp.cos(col_ang)] * 2, axis=-1)
    sin = jnp.concatenate([-jnp.sin(row_ang), jnp.sin(row_ang),
                           -jnp.sin(col_ang), jnp.sin(col_ang)], axis=-1)
    return jnp.tile(cos, (1, 2)), jnp.tile(sin, (1, 2))


def _layer(x2, batch, seq, norm_gain, w_in, q_gain, k_gain, w_proj_attn, ln_v_gain, ln_v_bias,
           w_spatial, b_spatial, w_proj_gmlp, b_merge, w_out, final_gain):
    n = batch * seq
    bf16 = jnp.bfloat16
    f32 = jnp.float32
    cos, sin = _rope_tables(seq)
    seg = np.arange(2 * LANES) // HEAD_DIM
    block_diag = seg[:, None] == seg[None, :]
    ones_bd = jnp.asarray(np.concatenate([block_diag, block_diag], axis=0), dtype=bf16)
    row = lambda v: v.reshape(1, -1).astype(f32)
    const = lambda shape: pl.BlockSpec(shape, lambda *_: (0,) * len(shape))
    in_width = w_in.shape[1]
    seq_blocks = seq // TM_IN

    qt, kx, vxt, rest = pl.pallas_call(
        _in_proj_kernel,
        grid=(n // TM_IN,),
        in_specs=[
            pl.BlockSpec((TM_IN, D_MODEL), lambda i: (i, 0)),
            const((1, D_MODEL)),
            const((D_MODEL, in_width)),
            const((1, LANES)),
            const((1, LANES)),
            pl.BlockSpec((TM_IN, LANES), lambda i: (i % seq_blocks, 0)),
            pl.BlockSpec((TM_IN, LANES), lambda i: (i % seq_blocks, 0)),
            const((4 * LANES, 2 * LANES)),
            const((1, GMLP_WIDTH)),
            const((1, GMLP_WIDTH)),
            const((1, N_BRANCHES * D_MODEL)),
        ],
        out_specs=[
            pl.BlockSpec((ATTN_WIDTH, TM_IN), lambda i: (0, i)),
            pl.BlockSpec((TM_IN, 4 * LANES), lambda i: (i, 0)),
            pl.BlockSpec((2 * LANES, TM_IN), lambda i: (0, i)),
            pl.BlockSpec((TM_IN, REST_WIDTH), lambda i: (i, 0)),
        ],
        out_shape=[
            jax.ShapeDtypeStruct((ATTN_WIDTH, n), bf16),
            jax.ShapeDtypeStruct((n, 4 * LANES), bf16),
            jax.ShapeDtypeStruct((2 * LANES, n), bf16),
            jax.ShapeDtypeStruct((n, REST_WIDTH), bf16),
        ],
        compiler_params=pltpu.CompilerParams(
            dimension_semantics=("arbitrary",), vmem_limit_bytes=VMEM_LIMIT_BYTES),
        name="in_proj",
    )(x2, row(norm_gain), w_in.astype(bf16), row(jnp.tile(q_gain, 2)), row(jnp.tile(k_gain, 2)),
      cos, sin, ones_bd, row(ln_v_gain), row(ln_v_bias), row(b_merge))

    kv_width = GQA_GROUP * HEAD_DIM
    attn = pl.pallas_call(
        _attn_kernel,
        grid=(batch, N_KV_HEADS),
        in_specs=[
            pl.BlockSpec((kv_width, seq), lambda b, g: (g, b)),
            pl.BlockSpec((seq, 2 * LANES), lambda b, g: (b, g)),
            pl.BlockSpec((LANES, seq), lambda b, g: (g, b)),
        ],
        out_specs=pl.BlockSpec((seq, kv_width), lambda b, g: (b, g)),
        out_shape=jax.ShapeDtypeStruct((n, ATTN_WIDTH), f32),
        scratch_shapes=[
            pltpu.VMEM((GQA_GROUP, 8, TQ), f32),
            pltpu.VMEM((GQA_GROUP, LANES, TQ), f32),
            pltpu.VMEM((GQA_GROUP, CK, TQ), f32),
            pltpu.VMEM((GQA_GROUP, CK, TQ), f32),
            pltpu.VMEM((GQA_GROUP, 8, TQ), f32),
            pltpu.VMEM((GQA_GROUP, 8, TQ), f32),
        ],
        compiler_params=pltpu.CompilerParams(
            dimension_semantics=("arbitrary", "arbitrary"),
            vmem_limit_bytes=VMEM_LIMIT_BYTES),
        name="attention",
    )(qt, kx, vxt)

    ws_stack = w_spatial.reshape(N_GROUPS // 2, 2, CHUNK, CHUNK).transpose(0, 2, 1, 3).reshape(
        N_GROUPS // 2, CHUNK, 2 * CHUNK).astype(bf16)
    b_sp = jnp.repeat(b_spatial.T, GROUP_DIM, axis=1).astype(f32)
    out = pl.pallas_call(
        _post_kernel,
        grid=(n // TM_POST,),
        in_specs=[
            pl.BlockSpec((TM_POST, ATTN_WIDTH), lambda i: (i, 0)),
            pl.BlockSpec((TM_POST, REST_WIDTH), lambda i: (i, 0)),
            pl.BlockSpec((TM_POST, D_MODEL), lambda i: (i, 0)),
            const((ATTN_WIDTH, D_MODEL)),
            const((GMLP_WIDTH, D_MODEL)),
            const((D_MODEL, D_MODEL)),
            const((N_GROUPS // 2, CHUNK, 2 * CHUNK)),
            const((CHUNK, GMLP_WIDTH)),
            const((1, D_MODEL)),
        ],
        out_specs=pl.BlockSpec((TM_POST, D_MODEL), lambda i: (i, 0)),
        out_shape=jax.ShapeDtypeStruct((n, D_MODEL), f32),
        scratch_shapes=[pltpu.VMEM((TM_POST, GMLP_WIDTH), f32)],
        compiler_params=pltpu.CompilerParams(
            dimension_semantics=("arbitrary",), vmem_limit_bytes=VMEM_LIMIT_BYTES),
        name="post",
    )(attn, rest, x2, w_proj_attn.astype(bf16), w_proj_gmlp.astype(bf16), w_out.astype(bf16),
      ws_stack, b_sp, row(final_gain))
    return out


def kernel(x, norm_gain, w_in, q_gain, k_gain, w_proj_attn, ln_v_gain, ln_v_bias, w_spatial,
           b_spatial, w_proj_gmlp, b_merge, w_out, final_gain):
    batch, seq, d = x.shape
    depth = w_in.shape[0]
    assert depth == 1, "the final rmsnorm is fused into the single layer's last call"
    assert d == D_MODEL and seq % TM_IN == 0 and seq % TQ == 0 and seq % CK == 0
    out = _layer(x.reshape(batch * seq, d), batch, seq, norm_gain[0], w_in[0], q_gain[0],
                 k_gain[0], w_proj_attn[0], ln_v_gain[0], ln_v_bias[0], w_spatial[0],
                 b_spatial[0], w_proj_gmlp[0], b_merge[0], w_out[0], final_gain)
    return out.reshape(batch, seq, d)
```

```python
import functools
import math

import jax
import jax.numpy as jnp
import numpy as np
from jax.experimental import pallas as pl
from jax.experimental.pallas import tpu as pltpu

D_MODEL = 1024
GRID_W = 64
HEAD_DIM = 64
N_Q_HEADS = 8
N_KV_HEADS = 2
GQA_GROUP = N_Q_HEADS // N_KV_HEADS
ATTN_WIDTH = N_Q_HEADS * HEAD_DIM
KV_WIDTH = N_KV_HEADS * HEAD_DIM
AXIAL_DIM = HEAD_DIM // 2
ROPE_THETA = 10000.0
CHUNK = 128
N_GROUPS = 8
GMLP_WIDTH = 512
GROUP_DIM = GMLP_WIDTH // N_GROUPS
N_BRANCHES = 2
EPS = 1e-6
REST_WIDTH = ATTN_WIDTH + 3 * GMLP_WIDTH + N_BRANCHES * D_MODEL

LANES = 128
VMEM_LIMIT_BYTES = 56 * 1024 * 1024

TM_IN = 512
REST_STEP = 512
TQ = 256
CK = 512
STEPS_PER_BODY = 6
V_ROWS = 80
TM_POST = 512
NEG_BIG = -1e30


def _segment_sums(z, bd_stack):
    sq = z * z
    hi = sq.astype(jnp.bfloat16)
    lo = (sq - hi.astype(jnp.float32)).astype(jnp.bfloat16)
    return jnp.dot(jnp.concatenate([hi, lo], axis=1), bd_stack,
                   preferred_element_type=jnp.float32)


def _head_norm_rope(z, sums, gain, cos, sin_signed, scale):
    ms = sums * (1.0 / HEAD_DIM)
    y = z * (jax.lax.rsqrt(ms + EPS) * scale) * gain
    lane = jax.lax.broadcasted_iota(jnp.int32, y.shape, 1)
    half = AXIAL_DIM // 2
    partner = jnp.where((lane & half) != 0,
                        pltpu.roll(y, half, 1),
                        pltpu.roll(y, LANES - half, 1))
    return y * cos + partner * sin_signed


def _in_proj_kernel(x_ref, ng_ref, w_ref, qg_ref, kg_ref, cos_ref, sin_ref, bd_ref,
                    lng_ref, lnb_ref, bm_ref, qt_ref, kx_ref, vxt_ref, rest_ref):
    f32 = jnp.float32
    x = x_ref[...]
    ms = jnp.mean(x * x, axis=-1, keepdims=True)
    h = x * jax.lax.rsqrt(ms + EPS) * ng_ref[...]

    def layernorm_v(z):
        g = jax.nn.gelu(z)
        mu = jnp.mean(g, axis=-1, keepdims=True)
        var = jnp.mean(jnp.square(g - mu), axis=-1, keepdims=True)
        return (g - mu) * jax.lax.rsqrt(var + EPS) * lng_ref[...] + lnb_ref[...]

    def merge_gate(c):
        lo = (c - 4) * REST_STEP
        return lambda z: jax.nn.sigmoid(z + bm_ref[:, lo:lo + REST_STEP])

    activations = [jax.nn.silu, jax.nn.gelu, layernorm_v, jax.nn.silu] + [
        merge_gate(c) for c in range(4, REST_WIDTH // REST_STEP)]

    def rest_chunks(lo, hi):
        base = ATTN_WIDTH + 2 * KV_WIDTH
        for c in range(lo, hi):
            z = jnp.dot(h, w_ref[:, base + c * REST_STEP:base + (c + 1) * REST_STEP],
                        preferred_element_type=f32)
            rest_ref[:, c * REST_STEP:(c + 1) * REST_STEP] = activations[c](z).astype(rest_ref.dtype)

    n_rest = REST_WIDTH // REST_STEP
    zq = jnp.dot(h, w_ref[:, 0:ATTN_WIDTH], preferred_element_type=f32)
    zkv = jnp.dot(h, w_ref[:, ATTN_WIDTH:ATTN_WIDTH + 2 * KV_WIDTH], preferred_element_type=f32)
    rest_chunks(0, n_rest // 2)
    bd_stack = bd_ref[...]
    sums_q = [_segment_sums(zq[:, i * 2 * LANES:(i + 1) * 2 * LANES], bd_stack)
              for i in range(ATTN_WIDTH // (2 * LANES))]
    sums_kv = _segment_sums(zkv, bd_stack)
    rest_chunks(n_rest // 2, n_rest)

    cos = cos_ref[...]
    sin_signed = sin_ref[...]
    lane = jax.lax.broadcasted_iota(jnp.int32, (x.shape[0], LANES), 1)
    low = lane < HEAD_DIM

    scale = math.log2(math.e) / math.sqrt(HEAD_DIM)
    for j in range(ATTN_WIDTH // LANES):
        sums = sums_q[j // 2][:, (j % 2) * LANES:(j % 2 + 1) * LANES]
        q = _head_norm_rope(zq[:, j * LANES:(j + 1) * LANES], sums, qg_ref[...],
                            cos, sin_signed, scale)
        qt_ref[j * LANES:(j + 1) * LANES, :] = q.T.astype(qt_ref.dtype)

    k = _head_norm_rope(zkv[:, 0:LANES], sums_kv[:, 0:LANES], kg_ref[...], cos, sin_signed, 1.0)
    k_sw = pltpu.roll(k, HEAD_DIM, 1)
    zero = jnp.zeros_like(k)
    kx_ref[:, 0 * LANES:1 * LANES] = jnp.where(low, k, zero).astype(kx_ref.dtype)
    kx_ref[:, 1 * LANES:2 * LANES] = jnp.where(low, zero, k_sw).astype(kx_ref.dtype)
    kx_ref[:, 2 * LANES:3 * LANES] = jnp.where(low, k_sw, zero).astype(kx_ref.dtype)
    kx_ref[:, 3 * LANES:4 * LANES] = jnp.where(low, zero, k).astype(kx_ref.dtype)

    v = zkv[:, LANES:2 * LANES]
    v_sw = pltpu.roll(v, HEAD_DIM, 1)
    ones_col = jnp.where(lane == HEAD_DIM, 1.0, 0.0)
    vxt_ref[0 * LANES:1 * LANES, :] = jnp.where(low, v, ones_col).T.astype(vxt_ref.dtype)
    vxt_ref[1 * LANES:2 * LANES, :] = jnp.where(low, v_sw, ones_col).T.astype(vxt_ref.dtype)


def _attn_kernel(qt_ref, kx_ref, vxt_ref, o_ref, m_ref, acc_ref, s0_ref, s1_ref, mc0_ref, mc1_ref):
    n_heads = GQA_GROUP
    seq = kx_ref.shape[0]
    n_chunks = seq // CK
    n_qblocks = seq // TQ
    bufs = ((s0_ref, mc0_ref), (s1_ref, mc1_ref))
    assert n_chunks % 2 == 0 and STEPS_PER_BODY % 2 == 0

    def offset(i, size):
        return i * size if isinstance(i, int) else pl.multiple_of(i * size, size)

    def scores(qb, c, h, s_ref, mc_ref):
        qt = qt_ref[(h // 2) * LANES:(h // 2 + 1) * LANES, pl.ds(offset(qb, TQ), TQ)]
        kv = kx_ref[pl.ds(offset(c, CK), CK), (h % 2) * LANES:(h % 2 + 1) * LANES]
        s = jnp.dot(kv, qt, preferred_element_type=jnp.float32)
        s_ref[h] = s
        mc_ref[h] = jnp.broadcast_to(jnp.max(s, axis=0, keepdims=True), mc_ref.shape[1:])

    def accumulate(c, h, s_ref, mc_ref):
        vxt = vxt_ref[0:V_ROWS, pl.ds(offset(c, CK), CK)]
        m_old = m_ref[h]
        m_new = jnp.maximum(m_old, mc_ref[h])
        alpha = jnp.exp2(m_old - m_new)
        p = jnp.exp2(s_ref[h] - m_new[0:1, :]).astype(jnp.bfloat16)
        acc_ref[h, 0:V_ROWS, :] = acc_ref[h, 0:V_ROWS, :] * alpha[0:1, :] + jnp.dot(
            vxt, p, preferred_element_type=jnp.float32)
        m_ref[h] = m_new

    def step(qb, c, parity, qb_next, c_next):
        for h in range(n_heads):
            scores(qb_next, c_next, h, *bufs[1 - parity])
            accumulate(c, h, *bufs[parity])

    def finalize(qb):
        lane = jax.lax.broadcasted_iota(jnp.int32, (TQ, LANES), 1)
        for j in range(n_heads // 2):
            halves = []
            for h in (2 * j, 2 * j + 1):
                acc = acc_ref[h]
                halves.append((acc * (1.0 / acc[HEAD_DIM:HEAD_DIM + 1, :])).T)
            pair = jnp.where(lane < HEAD_DIM, halves[0], pltpu.roll(halves[1], HEAD_DIM, 1))
            o_ref[pl.ds(offset(qb, TQ), TQ), j * LANES:(j + 1) * LANES] = pair.astype(o_ref.dtype)
        m_ref[...] = jnp.full(m_ref.shape, NEG_BIG, jnp.float32)

    m_ref[...] = jnp.full(m_ref.shape, NEG_BIG, jnp.float32)
    acc_ref[...] = jnp.zeros(acc_ref.shape, jnp.float32)
    for h in range(n_heads):
        scores(0, 0, h, *bufs[0])

    n_loop = (n_chunks - 1) // STEPS_PER_BODY

    def qblock(qb, carry):
        def body(i, carry):
            for u in range(STEPS_PER_BODY):
                c = STEPS_PER_BODY * i + u
                step(qb, c, u % 2, qb, c + 1)
            return carry

        jax.lax.fori_loop(0, n_loop, body, 0)
        for c in range(n_loop * STEPS_PER_BODY, n_chunks - 1):
            step(qb, c, c % 2, qb, c + 1)
        step(qb, n_chunks - 1, 1, jnp.minimum(qb + 1, n_qblocks - 1), 0)
        finalize(qb)
        return carry

    jax.lax.fori_loop(0, n_qblocks, qblock, 0)


def _post_kernel(attn_ref, rest_ref, x_ref, wpa_ref, wpg_ref, wout_ref, ws_ref,
                 bsp_ref, fg_ref, o_ref, vmix_ref):
    f32 = jnp.float32
    tm = x_ref.shape[0]

    def rest(lo, hi):
        return rest_ref[:, lo:hi].astype(f32)

    a = attn_ref[...] * rest(0, ATTN_WIDTH)
    y_a = jnp.dot(a, wpa_ref[...], preferred_element_type=f32)

    o0 = ATTN_WIDTH
    vv = rest_ref[:, o0 + GMLP_WIDTH:o0 + 2 * GMLP_WIDTH]
    low = jax.lax.broadcasted_iota(jnp.int32, (CHUNK, LANES), 1) < GROUP_DIM
    zero = jnp.zeros((CHUNK, LANES), jnp.bfloat16)
    for c in range(tm // CHUNK):
        slabs = []
        for j in range(GMLP_WIDTH // LANES):
            v_slab = vv[c * CHUNK:(c + 1) * CHUNK, j * LANES:(j + 1) * LANES]
            rhs = jnp.concatenate([jnp.where(low, v_slab, zero), jnp.where(low, zero, v_slab)],
                                  axis=0)
            slabs.append(jnp.dot(ws_ref[j], rhs, preferred_element_type=f32))
        vmix_ref[c * CHUNK:(c + 1) * CHUNK, :] = jnp.concatenate(slabs, axis=1) + bsp_ref[...]
    gm = rest(o0, o0 + GMLP_WIDTH) * vmix_ref[...] * rest(o0 + 2 * GMLP_WIDTH, o0 + 3 * GMLP_WIDTH)
    y_b = jnp.dot(gm, wpg_ref[...], preferred_element_type=f32)

    o1 = o0 + 3 * GMLP_WIDTH
    y = rest(o1, o1 + D_MODEL) * y_a + rest(o1 + D_MODEL, o1 + 2 * D_MODEL) * y_b
    out = x_ref[...] + jnp.dot(y, wout_ref[...], preferred_element_type=f32)
    ms = jnp.mean(out * out, axis=-1, keepdims=True)
    o_ref[...] = out * jax.lax.rsqrt(ms + EPS) * fg_ref[...]


def _rope_tables(seq):
    token = np.arange(seq)
    inv_freq = ROPE_THETA ** (-np.arange(0, AXIAL_DIM, 2, dtype=np.float64) / AXIAL_DIM)
    row_ang = (token // GRID_W)[:, None] * inv_freq
    col_ang = (token % GRID_W)[:, None] * inv_freq
    cos = np.concatenate([np.cos(row_ang)] * 2 + [np.cos(col_ang)] * 2, axis=-1)
    sin = np.concatenate([-np.sin(row_ang), np.sin(row_ang),
                          -np.sin(col_ang), np.sin(col_ang)], axis=-1)
    return (jnp.asarray(np.tile(cos, (1, 2)), jnp.float32),
            jnp.asarray(np.tile(sin, (1, 2)), jnp.float32))


def _layer(x2, batch, seq, norm_gain, w_in, q_gain, k_gain, w_proj_attn, ln_v_gain, ln_v_bias,
           w_spatial, b_spatial, w_proj_gmlp, b_merge, w_out, final_gain):
    n = batch * seq
    bf16 = jnp.bfloat16
    f32 = jnp.float32
    cos, sin = _rope_tables(seq)
    seg = np.arange(2 * LANES) // HEAD_DIM
    block_diag = seg[:, None] == seg[None, :]
    ones_bd = jnp.asarray(np.concatenate([block_diag, block_diag], axis=0), dtype=bf16)
    row = lambda v: v.reshape(1, -1).astype(f32)
    const = lambda shape: pl.BlockSpec(shape, lambda *_: (0,) * len(shape))
    in_width = w_in.shape[1]
    seq_blocks = seq // TM_IN

    qt, kx, vxt, rest = pl.pallas_call(
        _in_proj_kernel,
        grid=(n // TM_IN,),
        in_specs=[
            pl.BlockSpec((TM_IN, D_MODEL), lambda i: (i, 0)),
            const((1, D_MODEL)),
            pl.BlockSpec((D_MODEL, in_width), lambda i: (0, 0), pipeline_mode=pl.Buffered(1)),
            const((1, LANES)),
            const((1, LANES)),
            pl.BlockSpec((TM_IN, LANES), lambda i: (i % seq_blocks, 0)),
            pl.BlockSpec((TM_IN, LANES), lambda i: (i % seq_blocks, 0)),
            const((4 * LANES, 2 * LANES)),
            const((1, GMLP_WIDTH)),
            const((1, GMLP_WIDTH)),
            const((1, N_BRANCHES * D_MODEL)),
        ],
        out_specs=[
            pl.BlockSpec((ATTN_WIDTH, TM_IN), lambda i: (0, i)),
            pl.BlockSpec((TM_IN, 4 * LANES), lambda i: (i, 0)),
            pl.BlockSpec((2 * LANES, TM_IN), lambda i: (0, i)),
            pl.BlockSpec((TM_IN, REST_WIDTH), lambda i: (i, 0)),
        ],
        out_shape=[
            jax.ShapeDtypeStruct((ATTN_WIDTH, n), bf16),
            jax.ShapeDtypeStruct((n, 4 * LANES), bf16),
            jax.ShapeDtypeStruct((2 * LANES, n), bf16),
            jax.ShapeDtypeStruct((n, REST_WIDTH), bf16),
        ],
        compiler_params=pltpu.CompilerParams(
            dimension_semantics=("arbitrary",), vmem_limit_bytes=VMEM_LIMIT_BYTES),
        name="in_proj",
    )(x2, row(norm_gain), w_in, row(jnp.tile(q_gain, 2)), row(jnp.tile(k_gain, 2)),
      cos, sin, ones_bd, row(ln_v_gain), row(ln_v_bias), row(b_merge))

    kv_width = GQA_GROUP * HEAD_DIM
    attn = pl.pallas_call(
        _attn_kernel,
        grid=(batch, N_KV_HEADS),
        in_specs=[
            pl.BlockSpec((kv_width, seq), lambda b, g: (g, b)),
            pl.BlockSpec((seq, 2 * LANES), lambda b, g: (b, g)),
            pl.BlockSpec((LANES, seq), lambda b, g: (g, b)),
        ],
        out_specs=pl.BlockSpec((seq, kv_width), lambda b, g: (b, g)),
        out_shape=jax.ShapeDtypeStruct((n, ATTN_WIDTH), f32),
        scratch_shapes=[
            pltpu.VMEM((GQA_GROUP, 8, TQ), f32),
            pltpu.VMEM((GQA_GROUP, LANES, TQ), f32),
            pltpu.VMEM((GQA_GROUP, CK, TQ), f32),
            pltpu.VMEM((GQA_GROUP, CK, TQ), f32),
            pltpu.VMEM((GQA_GROUP, 8, TQ), f32),
            pltpu.VMEM((GQA_GROUP, 8, TQ), f32),
        ],
        compiler_params=pltpu.CompilerParams(
            dimension_semantics=("arbitrary", "arbitrary"),
            vmem_limit_bytes=VMEM_LIMIT_BYTES),
        name="attention",
    )(qt, kx, vxt)

    ws_stack = w_spatial.reshape(N_GROUPS // 2, 2, CHUNK, CHUNK).transpose(0, 2, 1, 3).reshape(
        N_GROUPS // 2, CHUNK, 2 * CHUNK).astype(bf16)
    b_sp = jnp.repeat(b_spatial.T, GROUP_DIM, axis=1).astype(f32)
    out = pl.pallas_call(
        _post_kernel,
        grid=(n // TM_POST,),
        in_specs=[
            pl.BlockSpec((TM_POST, ATTN_WIDTH), lambda i: (i, 0)),
            pl.BlockSpec((TM_POST, REST_WIDTH), lambda i: (i, 0)),
            pl.BlockSpec((TM_POST, D_MODEL), lambda i: (i, 0)),
            const((ATTN_WIDTH, D_MODEL)),
            const((GMLP_WIDTH, D_MODEL)),
            const((D_MODEL, D_MODEL)),
            const((N_GROUPS // 2, CHUNK, 2 * CHUNK)),
            const((CHUNK, GMLP_WIDTH)),
            const((1, D_MODEL)),
        ],
        out_specs=pl.BlockSpec((TM_POST, D_MODEL), lambda i: (i, 0)),
        out_shape=jax.ShapeDtypeStruct((n, D_MODEL), f32),
        scratch_shapes=[pltpu.VMEM((TM_POST, GMLP_WIDTH), f32)],
        compiler_params=pltpu.CompilerParams(
            dimension_semantics=("arbitrary",), vmem_limit_bytes=VMEM_LIMIT_BYTES),
        name="post",
    )(attn, rest, x2, w_proj_attn, w_proj_gmlp, w_out,
      ws_stack, b_sp, row(final_gain))
    return out


def kernel(x, norm_gain, w_in, q_gain, k_gain, w_proj_attn, ln_v_gain, ln_v_bias, w_spatial,
           b_spatial, w_proj_gmlp, b_merge, w_out, final_gain):
    batch, seq, d = x.shape
    depth = w_in.shape[0]
    assert depth == 1, "the final rmsnorm is fused into the single layer's last call"
    assert d == D_MODEL and seq % TM_IN == 0 and seq % TQ == 0 and seq % CK == 0
    out = _layer(x.reshape(batch * seq, d), batch, seq, norm_gain[0], w_in[0], q_gain[0],
                 k_gain[0], w_proj_attn[0], ln_v_gain[0], ln_v_bias[0], w_spatial[0],
                 b_spatial[0], w_proj_gmlp[0], b_merge[0], w_out[0], final_gain)
    return out.reshape(batch, seq, d)
```

```python
import functools
import math

import jax
import jax.numpy as jnp
import numpy as np
from jax.experimental import pallas as pl
from jax.experimental.pallas import tpu as pltpu

D_MODEL = 1024
GRID_W = 64
HEAD_DIM = 64
N_Q_HEADS = 8
N_KV_HEADS = 2
GQA_GROUP = N_Q_HEADS // N_KV_HEADS
ATTN_WIDTH = N_Q_HEADS * HEAD_DIM
KV_WIDTH = N_KV_HEADS * HEAD_DIM
AXIAL_DIM = HEAD_DIM // 2
ROPE_THETA = 10000.0
CHUNK = 128
N_GROUPS = 8
GMLP_WIDTH = 512
GROUP_DIM = GMLP_WIDTH // N_GROUPS
N_BRANCHES = 2
EPS = 1e-6
REST_WIDTH = ATTN_WIDTH + 3 * GMLP_WIDTH + N_BRANCHES * D_MODEL

LANES = 128
VMEM_LIMIT_BYTES = 56 * 1024 * 1024

TM_IN = 512
REST_STEP = 512
QKV_AFTER_CHUNKS = 5
SEGSUM_AFTER_CHUNKS = 7
TQ = 256
CK = 512
STEPS_PER_BODY = 6
V_ROWS = 72
TM_POST = 512
NEG_BIG = -1e30


def _segment_sums(z, bd_stack):
    sq = z * z
    hi = sq.astype(jnp.bfloat16)
    lo = (sq - hi.astype(jnp.float32)).astype(jnp.bfloat16)
    return jnp.dot(jnp.concatenate([hi, lo], axis=1), bd_stack,
                   preferred_element_type=jnp.float32)


def _head_norm_rope(z, sums, gain, cos, sin_signed, scale):
    ms = sums * (1.0 / HEAD_DIM)
    y = z * (jax.lax.rsqrt(ms + EPS) * scale) * gain
    lane = jax.lax.broadcasted_iota(jnp.int32, y.shape, 1)
    half = AXIAL_DIM // 2
    partner = jnp.where((lane & half) != 0,
                        pltpu.roll(y, half, 1),
                        pltpu.roll(y, LANES - half, 1))
    return y * cos + partner * sin_signed


def _in_proj_kernel(x_ref, ng_ref, w_ref, qg_ref, kg_ref, cos_ref, sin_ref, bd_ref,
                    lng_ref, lnb_ref, bm_ref, qt_ref, kx_ref, vxt_ref, rest_ref):
    f32 = jnp.float32
    x = x_ref[...]
    ms = jnp.mean(x * x, axis=-1, keepdims=True)
    h = x * jax.lax.rsqrt(ms + EPS) * ng_ref[...]

    def project(lo, hi):
        return jnp.dot(h, w_ref[:, lo:hi], preferred_element_type=f32)

    def layernorm_v(z):
        g = jax.nn.gelu(z)
        mu = jnp.mean(g, axis=-1, keepdims=True)
        var = jnp.mean(jnp.square(g - mu), axis=-1, keepdims=True)
        return (g - mu) * jax.lax.rsqrt(var + EPS) * lng_ref[...] + lnb_ref[...]

    def merge_gate(c):
        lo = (c - 4) * REST_STEP
        return lambda z: jax.nn.sigmoid(z + bm_ref[:, lo:lo + REST_STEP])

    activations = [jax.nn.silu, jax.nn.gelu, layernorm_v, jax.nn.silu] + [
        merge_gate(c) for c in range(4, REST_WIDTH // REST_STEP)]

    def rest_chunks(lo, hi):
        base = ATTN_WIDTH + 2 * KV_WIDTH
        for c in range(lo, hi):
            z = project(base + c * REST_STEP, base + (c + 1) * REST_STEP)
            rest_ref[:, c * REST_STEP:(c + 1) * REST_STEP] = activations[c](z).astype(rest_ref.dtype)

    n_rest = REST_WIDTH // REST_STEP
    rest_chunks(0, QKV_AFTER_CHUNKS)
    zq = project(0, ATTN_WIDTH)
    zkv = project(ATTN_WIDTH, ATTN_WIDTH + 2 * KV_WIDTH)
    rest_chunks(QKV_AFTER_CHUNKS, SEGSUM_AFTER_CHUNKS)
    bd_stack = bd_ref[...]
    sums_q = [_segment_sums(zq[:, i * 2 * LANES:(i + 1) * 2 * LANES], bd_stack)
              for i in range(ATTN_WIDTH // (2 * LANES))]
    sums_kv = _segment_sums(zkv, bd_stack)
    rest_chunks(SEGSUM_AFTER_CHUNKS, n_rest)

    cos = cos_ref[...]
    sin_signed = sin_ref[...]
    lane = jax.lax.broadcasted_iota(jnp.int32, (x.shape[0], LANES), 1)
    low = lane < HEAD_DIM

    scale = math.log2(math.e) / math.sqrt(HEAD_DIM)
    for j in range(ATTN_WIDTH // LANES):
        sums = sums_q[j // 2][:, (j % 2) * LANES:(j % 2 + 1) * LANES]
        q = _head_norm_rope(zq[:, j * LANES:(j + 1) * LANES], sums, qg_ref[...],
                            cos, sin_signed, scale)
        qt_ref[j * LANES:(j + 1) * LANES, :] = q.T.astype(qt_ref.dtype)

    k = _head_norm_rope(zkv[:, 0:LANES], sums_kv[:, 0:LANES], kg_ref[...], cos, sin_signed, 1.0)
    k_sw = pltpu.roll(k, HEAD_DIM, 1)
    zero = jnp.zeros_like(k)
    kx_ref[:, 0 * LANES:1 * LANES] = jnp.where(low, k, zero).astype(kx_ref.dtype)
    kx_ref[:, 1 * LANES:2 * LANES] = jnp.where(low, zero, k_sw).astype(kx_ref.dtype)
    kx_ref[:, 2 * LANES:3 * LANES] = jnp.where(low, k_sw, zero).astype(kx_ref.dtype)
    kx_ref[:, 3 * LANES:4 * LANES] = jnp.where(low, zero, k).astype(kx_ref.dtype)

    v = zkv[:, LANES:2 * LANES]
    v_sw = pltpu.roll(v, HEAD_DIM, 1)
    ones_col = jnp.where(lane == HEAD_DIM, 1.0, 0.0)
    vxt_ref[0 * LANES:1 * LANES, :] = jnp.where(low, v, ones_col).T.astype(vxt_ref.dtype)
    vxt_ref[1 * LANES:2 * LANES, :] = jnp.where(low, v_sw, ones_col).T.astype(vxt_ref.dtype)


def _attn_kernel(qt_ref, kx_ref, vxt_ref, o_ref, m_ref, acc_ref, s0_ref, s1_ref, mc0_ref, mc1_ref):
    n_heads = GQA_GROUP
    seq = kx_ref.shape[0]
    n_chunks = seq // CK
    n_qblocks = seq // TQ
    bufs = ((s0_ref, mc0_ref), (s1_ref, mc1_ref))
    assert n_chunks % 2 == 0 and STEPS_PER_BODY % 2 == 0

    def offset(i, size):
        return i * size if isinstance(i, int) else pl.multiple_of(i * size, size)

    def scores(qb, c, h, s_ref, mc_ref):
        qt = qt_ref[(h // 2) * LANES:(h // 2 + 1) * LANES, pl.ds(offset(qb, TQ), TQ)]
        kv = kx_ref[pl.ds(offset(c, CK), CK), (h % 2) * LANES:(h % 2 + 1) * LANES]
        s = jnp.dot(kv, qt, preferred_element_type=jnp.float32)
        s_ref[h] = s
        mc_ref[h] = jnp.broadcast_to(jnp.max(s, axis=0, keepdims=True), mc_ref.shape[1:])

    def accumulate(c, h, s_ref, mc_ref):
        vxt = vxt_ref[0:V_ROWS, pl.ds(offset(c, CK), CK)]
        m_old = m_ref[h]
        m_new = jnp.maximum(m_old, mc_ref[h])
        alpha = jnp.exp2(m_old - m_new)
        p = jnp.exp2(s_ref[h] - m_new[0:1, :])
        acc_ref[h, 0:V_ROWS, :] = acc_ref[h, 0:V_ROWS, :] * alpha[0:1, :] + jnp.dot(
            vxt, p, preferred_element_type=jnp.float32)
        m_ref[h] = m_new

    def step(qb, c, parity, qb_next, c_next):
        for h in range(n_heads):
            scores(qb_next, c_next, h, *bufs[1 - parity])
            accumulate(c, h, *bufs[parity])

    def finalize(qb):
        lane = jax.lax.broadcasted_iota(jnp.int32, (TQ, LANES), 1)
        for j in range(n_heads // 2):
            halves = []
            for h in (2 * j, 2 * j + 1):
                acc = acc_ref[h]
                halves.append((acc * (1.0 / acc[HEAD_DIM:HEAD_DIM + 1, :])).T)
            pair = jnp.where(lane < HEAD_DIM, halves[0], pltpu.roll(halves[1], HEAD_DIM, 1))
            o_ref[pl.ds(offset(qb, TQ), TQ), j * LANES:(j + 1) * LANES] = pair.astype(o_ref.dtype)
        m_ref[...] = jnp.full(m_ref.shape, NEG_BIG, jnp.float32)

    m_ref[...] = jnp.full(m_ref.shape, NEG_BIG, jnp.float32)
    acc_ref[...] = jnp.zeros(acc_ref.shape, jnp.float32)
    for h in range(n_heads):
        scores(0, 0, h, *bufs[0])

    n_loop = (n_chunks - 1) // STEPS_PER_BODY

    def qblock(qb, carry):
        def body(i, carry):
            for u in range(STEPS_PER_BODY):
                c = STEPS_PER_BODY * i + u
                step(qb, c, u % 2, qb, c + 1)
            return carry

        jax.lax.fori_loop(0, n_loop, body, 0)
        for c in range(n_loop * STEPS_PER_BODY, n_chunks - 1):
            step(qb, c, c % 2, qb, c + 1)
        step(qb, n_chunks - 1, 1, jnp.minimum(qb + 1, n_qblocks - 1), 0)
        finalize(qb)
        return carry

    jax.lax.fori_loop(0, n_qblocks, qblock, 0)


def _post_kernel(attn_ref, rest_ref, x_ref, wpa_ref, wpg_ref, wout_ref, ws_ref,
                 bsp_ref, fg_ref, o_ref, vmix_ref):
    f32 = jnp.float32
    tm = x_ref.shape[0]

    def rest(lo, hi):
        return rest_ref[:, lo:hi].astype(f32)

    a = attn_ref[...] * rest(0, ATTN_WIDTH)
    y_a = jnp.dot(a, wpa_ref[...], preferred_element_type=f32)

    o0 = ATTN_WIDTH
    vv = rest_ref[:, o0 + GMLP_WIDTH:o0 + 2 * GMLP_WIDTH]
    low = jax.lax.broadcasted_iota(jnp.int32, (CHUNK, LANES), 1) < GROUP_DIM
    zero = jnp.zeros((CHUNK, LANES), jnp.bfloat16)
    for c in range(tm // CHUNK):
        slabs = []
        for j in range(GMLP_WIDTH // LANES):
            v_slab = vv[c * CHUNK:(c + 1) * CHUNK, j * LANES:(j + 1) * LANES]
            rhs = jnp.concatenate([jnp.where(low, v_slab, zero), jnp.where(low, zero, v_slab)],
                                  axis=0)
            slabs.append(jnp.dot(ws_ref[j], rhs, preferred_element_type=f32))
        vmix_ref[c * CHUNK:(c + 1) * CHUNK, :] = jnp.concatenate(slabs, axis=1) + bsp_ref[...]
    gm = rest(o0, o0 + GMLP_WIDTH) * vmix_ref[...] * rest(o0 + 2 * GMLP_WIDTH, o0 + 3 * GMLP_WIDTH)
    y_b = jnp.dot(gm, wpg_ref[...], preferred_element_type=f32)

    o1 = o0 + 3 * GMLP_WIDTH
    y = rest(o1, o1 + D_MODEL) * y_a + rest(o1 + D_MODEL, o1 + 2 * D_MODEL) * y_b
    out = x_ref[...] + jnp.dot(y, wout_ref[...], preferred_element_type=f32)
    ms = jnp.mean(out * out, axis=-1, keepdims=True)
    o_ref[...] = out * jax.lax.rsqrt(ms + EPS) * fg_ref[...]


def _rope_tables(seq):
    token = np.arange(seq)
    inv_freq = ROPE_THETA ** (-np.arange(0, AXIAL_DIM, 2, dtype=np.float64) / AXIAL_DIM)
    row_ang = (token // GRID_W)[:, None] * inv_freq
    col_ang = (token % GRID_W)[:, None] * inv_freq
    cos = np.concatenate([np.cos(row_ang)] * 2 + [np.cos(col_ang)] * 2, axis=-1)
    sin = np.concatenate([-np.sin(row_ang), np.sin(row_ang),
                          -np.sin(col_ang), np.sin(col_ang)], axis=-1)
    return (jnp.asarray(np.tile(cos, (1, 2)), jnp.float32),
            jnp.asarray(np.tile(sin, (1, 2)), jnp.float32))


def _layer(x2, batch, seq, norm_gain, w_in, q_gain, k_gain, w_proj_attn, ln_v_gain, ln_v_bias,
           w_spatial, b_spatial, w_proj_gmlp, b_merge, w_out, final_gain):
    n = batch * seq
    bf16 = jnp.bfloat16
    f32 = jnp.float32
    cos, sin = _rope_tables(seq)
    seg = np.arange(2 * LANES) // HEAD_DIM
    block_diag = seg[:, None] == seg[None, :]
    ones_bd = jnp.asarray(np.concatenate([block_diag, block_diag], axis=0), dtype=bf16)
    row = lambda v: v.reshape(1, -1).astype(f32)
    const = lambda shape: pl.BlockSpec(shape, lambda *_: (0,) * len(shape))
    in_width = w_in.shape[1]
    seq_blocks = seq // TM_IN

    qt, kx, vxt, rest = pl.pallas_call(
        _in_proj_kernel,
        grid=(n // TM_IN,),
        in_specs=[
            pl.BlockSpec((TM_IN, D_MODEL), lambda i: (i, 0)),
            const((1, D_MODEL)),
            pl.BlockSpec((D_MODEL, in_width), lambda i: (0, 0), pipeline_mode=pl.Buffered(1)),
            const((1, LANES)),
            const((1, LANES)),
            pl.BlockSpec((TM_IN, LANES), lambda i: (i % seq_blocks, 0)),
            pl.BlockSpec((TM_IN, LANES), lambda i: (i % seq_blocks, 0)),
            const((4 * LANES, 2 * LANES)),
            const((1, GMLP_WIDTH)),
            const((1, GMLP_WIDTH)),
            const((1, N_BRANCHES * D_MODEL)),
        ],
        out_specs=[
            pl.BlockSpec((ATTN_WIDTH, TM_IN), lambda i: (0, i)),
            pl.BlockSpec((TM_IN, 4 * LANES), lambda i: (i, 0)),
            pl.BlockSpec((2 * LANES, TM_IN), lambda i: (0, i)),
            pl.BlockSpec((TM_IN, REST_WIDTH), lambda i: (i, 0)),
        ],
        out_shape=[
            jax.ShapeDtypeStruct((ATTN_WIDTH, n), bf16),
            jax.ShapeDtypeStruct((n, 4 * LANES), bf16),
            jax.ShapeDtypeStruct((2 * LANES, n), f32),
            jax.ShapeDtypeStruct((n, REST_WIDTH), bf16),
        ],
        compiler_params=pltpu.CompilerParams(
            dimension_semantics=("arbitrary",), vmem_limit_bytes=VMEM_LIMIT_BYTES),
        name="in_proj",
    )(x2, row(norm_gain), w_in, row(jnp.tile(q_gain, 2)), row(jnp.tile(k_gain, 2)),
      cos, sin, ones_bd, row(ln_v_gain), row(ln_v_bias), row(b_merge))

    kv_width = GQA_GROUP * HEAD_DIM
    attn = pl.pallas_call(
        _attn_kernel,
        grid=(batch, N_KV_HEADS),
        in_specs=[
            pl.BlockSpec((kv_width, seq), lambda b, g: (g, b)),
            pl.BlockSpec((seq, 2 * LANES), lambda b, g: (b, g)),
            pl.BlockSpec((LANES, seq), lambda b, g: (g, b)),
        ],
        out_specs=pl.BlockSpec((seq, kv_width), lambda b, g: (b, g)),
        out_shape=jax.ShapeDtypeStruct((n, ATTN_WIDTH), f32),
        scratch_shapes=[
            pltpu.VMEM((GQA_GROUP, 8, TQ), f32),
            pltpu.VMEM((GQA_GROUP, LANES, TQ), f32),
            pltpu.VMEM((GQA_GROUP, CK, TQ), f32),
            pltpu.VMEM((GQA_GROUP, CK, TQ), f32),
            pltpu.VMEM((GQA_GROUP, 8, TQ), f32),
            pltpu.VMEM((GQA_GROUP, 8, TQ), f32),
        ],
        compiler_params=pltpu.CompilerParams(
            dimension_semantics=("arbitrary", "arbitrary"),
            vmem_limit_bytes=VMEM_LIMIT_BYTES),
        name="attention",
    )(qt, kx, vxt)

    ws_stack = w_spatial.reshape(N_GROUPS // 2, 2, CHUNK, CHUNK).transpose(0, 2, 1, 3).reshape(
        N_GROUPS // 2, CHUNK, 2 * CHUNK).astype(bf16)
    b_sp = jnp.repeat(b_spatial.T, GROUP_DIM, axis=1).astype(f32)
    out = pl.pallas_call(
        _post_kernel,
        grid=(n // TM_POST,),
        in_specs=[
            pl.BlockSpec((TM_POST, ATTN_WIDTH), lambda i: (i, 0)),
            pl.BlockSpec((TM_POST, REST_WIDTH), lambda i: (i, 0)),
            pl.BlockSpec((TM_POST, D_MODEL), lambda i: (i, 0)),
            const((ATTN_WIDTH, D_MODEL)),
            const((GMLP_WIDTH, D_MODEL)),
            const((D_MODEL, D_MODEL)),
            const((N_GROUPS // 2, CHUNK, 2 * CHUNK)),
            const((CHUNK, GMLP_WIDTH)),
            const((1, D_MODEL)),
        ],
        out_specs=pl.BlockSpec((TM_POST, D_MODEL), lambda i: (i, 0)),
        out_shape=jax.ShapeDtypeStruct((n, D_MODEL), f32),
        scratch_shapes=[pltpu.VMEM((TM_POST, GMLP_WIDTH), f32)],
        compiler_params=pltpu.CompilerParams(
            dimension_semantics=("arbitrary",), vmem_limit_bytes=VMEM_LIMIT_BYTES),
        name="post",
    )(attn, rest, x2, w_proj_attn, w_proj_gmlp, w_out,
      ws_stack, b_sp, row(final_gain))
    return out


def kernel(x, norm_gain, w_in, q_gain, k_gain, w_proj_attn, ln_v_gain, ln_v_bias, w_spatial,
           b_spatial, w_proj_gmlp, b_merge, w_out, final_gain):
    batch, seq, d = x.shape
    depth = w_in.shape[0]
    assert depth == 1, "the final rmsnorm is fused into the single layer's last call"
    assert d == D_MODEL and seq % TM_IN == 0 and seq % TQ == 0 and seq % CK == 0
    out = _layer(x.reshape(batch * seq, d), batch, seq, norm_gain[0], w_in[0], q_gain[0],
                 k_gain[0], w_proj_attn[0], ln_v_gain[0], ln_v_bias[0], w_spatial[0],
                 b_spatial[0], w_proj_gmlp[0], b_merge[0], w_out[0], final_gain)
    return out.reshape(batch, seq, d)
```

```python
import functools
import math

import jax
import jax.numpy as jnp
import numpy as np
from jax.experimental import pallas as pl
from jax.experimental.pallas import tpu as pltpu

D_MODEL = 1024
GRID_W = 64
HEAD_DIM = 64
N_Q_HEADS = 8
N_KV_HEADS = 2
GQA_GROUP = N_Q_HEADS // N_KV_HEADS
ATTN_WIDTH = N_Q_HEADS * HEAD_DIM
KV_WIDTH = N_KV_HEADS * HEAD_DIM
AXIAL_DIM = HEAD_DIM // 2
ROPE_THETA = 10000.0
CHUNK = 128
N_GROUPS = 8
GMLP_WIDTH = 512
GROUP_DIM = GMLP_WIDTH // N_GROUPS
N_BRANCHES = 2
EPS = 1e-6
REST_WIDTH = 3 * GMLP_WIDTH + N_BRANCHES * D_MODEL

LANES = 128
VMEM_LIMIT_BYTES = 56 * 1024 * 1024

TM_IN = 512
REST_STEP = 512
QKV_AFTER_CHUNKS = 5
SEGSUM_AFTER_CHUNKS = 7
TQ = 256
CK = 512
STEPS_PER_BODY = 6
V_ROWS = 72
TM_POST = 512
NEG_BIG = -1e30


def _segment_sums(z, bd_stack):
    sq = z * z
    hi = sq.astype(jnp.bfloat16)
    lo = (sq - hi.astype(jnp.float32)).astype(jnp.bfloat16)
    return jnp.dot(jnp.concatenate([hi, lo], axis=1), bd_stack,
                   preferred_element_type=jnp.float32)


def _head_norm_rope(z, sums, gain, cos, sin_signed, scale):
    ms = sums * (1.0 / HEAD_DIM)
    y = z * (jax.lax.rsqrt(ms + EPS) * scale) * gain
    lane = jax.lax.broadcasted_iota(jnp.int32, y.shape, 1)
    half = AXIAL_DIM // 2
    partner = jnp.where((lane & half) != 0,
                        pltpu.roll(y, half, 1),
                        pltpu.roll(y, LANES - half, 1))
    return y * cos + partner * sin_signed


def _in_proj_kernel(x_ref, ng_ref, w_ref, qg_ref, kg_ref, cos_ref, sin_ref, bd_ref,
                    lng_ref, lnb_ref, bm_ref, qt_ref, kx_ref, vxt_ref, ga_ref, rest_ref):
    f32 = jnp.float32
    x = x_ref[...]
    ms = jnp.mean(x * x, axis=-1, keepdims=True)
    h = x * jax.lax.rsqrt(ms + EPS) * ng_ref[...]

    def project(lo, hi):
        return jnp.dot(h, w_ref[:, lo:hi], preferred_element_type=f32)

    def layernorm_v(z):
        g = jax.nn.gelu(z)
        mu = jnp.mean(g, axis=-1, keepdims=True)
        var = jnp.mean(jnp.square(g - mu), axis=-1, keepdims=True)
        return (g - mu) * jax.lax.rsqrt(var + EPS) * lng_ref[...] + lnb_ref[...]

    def merge_gate(c):
        lo = (c - 4) * REST_STEP
        return lambda z: jax.nn.sigmoid(z + bm_ref[:, lo:lo + REST_STEP])

    n_rest = (ATTN_WIDTH + REST_WIDTH) // REST_STEP
    activations = [jax.nn.silu, jax.nn.gelu, layernorm_v, jax.nn.silu] + [
        merge_gate(c) for c in range(4, n_rest)]

    def rest_chunks(lo, hi):
        base = ATTN_WIDTH + 2 * KV_WIDTH
        for c in range(lo, hi):
            z = project(base + c * REST_STEP, base + (c + 1) * REST_STEP)
            if c == 0:
                ga_ref[...] = activations[c](z).astype(ga_ref.dtype)
            else:
                rest_ref[:, (c - 1) * REST_STEP:c * REST_STEP] = activations[c](z).astype(
                    rest_ref.dtype)

    rest_chunks(0, QKV_AFTER_CHUNKS)
    zq = project(0, ATTN_WIDTH)
    zkv = project(ATTN_WIDTH, ATTN_WIDTH + 2 * KV_WIDTH)
    rest_chunks(QKV_AFTER_CHUNKS, SEGSUM_AFTER_CHUNKS)
    bd_stack = bd_ref[...]
    sums_q = [_segment_sums(zq[:, i * 2 * LANES:(i + 1) * 2 * LANES], bd_stack)
              for i in range(ATTN_WIDTH // (2 * LANES))]
    sums_kv = _segment_sums(zkv, bd_stack)
    rest_chunks(SEGSUM_AFTER_CHUNKS, n_rest)

    cos = cos_ref[...]
    sin_signed = sin_ref[...]
    lane = jax.lax.broadcasted_iota(jnp.int32, (x.shape[0], LANES), 1)
    low = lane < HEAD_DIM

    scale = math.log2(math.e) / math.sqrt(HEAD_DIM)
    for j in range(ATTN_WIDTH // LANES):
        sums = sums_q[j // 2][:, (j % 2) * LANES:(j % 2 + 1) * LANES]
        q = _head_norm_rope(zq[:, j * LANES:(j + 1) * LANES], sums, qg_ref[...],
                            cos, sin_signed, scale)
        qt_ref[j * LANES:(j + 1) * LANES, :] = q.T.astype(qt_ref.dtype)

    k = _head_norm_rope(zkv[:, 0:LANES], sums_kv[:, 0:LANES], kg_ref[...], cos, sin_signed, 1.0)
    k_sw = pltpu.roll(k, HEAD_DIM, 1)
    zero = jnp.zeros_like(k)
    kx_ref[:, 0 * LANES:1 * LANES] = jnp.where(low, k, zero).astype(kx_ref.dtype)
    kx_ref[:, 1 * LANES:2 * LANES] = jnp.where(low, zero, k_sw).astype(kx_ref.dtype)
    kx_ref[:, 2 * LANES:3 * LANES] = jnp.where(low, k_sw, zero).astype(kx_ref.dtype)
    kx_ref[:, 3 * LANES:4 * LANES] = jnp.where(low, zero, k).astype(kx_ref.dtype)

    v = zkv[:, LANES:2 * LANES]
    v_sw = pltpu.roll(v, HEAD_DIM, 1)
    ones_col = jnp.where(lane == HEAD_DIM, 1.0, 0.0)
    vxt_ref[0 * LANES:1 * LANES, :] = jnp.where(low, v, ones_col).T.astype(vxt_ref.dtype)
    vxt_ref[1 * LANES:2 * LANES, :] = jnp.where(low, v_sw, ones_col).T.astype(vxt_ref.dtype)


def _attn_kernel(qt_ref, kx_ref, vxt_ref, ga_ref, o_ref, m_ref, acc_ref, s0_ref, s1_ref, mc0_ref,
                 mc1_ref):
    n_heads = GQA_GROUP
    seq = kx_ref.shape[0]
    n_chunks = seq // CK
    n_qblocks = seq // TQ
    bufs = ((s0_ref, mc0_ref), (s1_ref, mc1_ref))
    assert n_chunks % 2 == 0 and STEPS_PER_BODY % 2 == 0

    def offset(i, size):
        return i * size if isinstance(i, int) else pl.multiple_of(i * size, size)

    def scores(qb, c, h, s_ref, mc_ref):
        qt = qt_ref[(h // 2) * LANES:(h // 2 + 1) * LANES, pl.ds(offset(qb, TQ), TQ)]
        kv = kx_ref[pl.ds(offset(c, CK), CK), (h % 2) * LANES:(h % 2 + 1) * LANES]
        s = jnp.dot(kv, qt, preferred_element_type=jnp.float32)
        s_ref[h] = s
        mc_ref[h] = jnp.broadcast_to(jnp.max(s, axis=0, keepdims=True), mc_ref.shape[1:])

    def accumulate(c, h, s_ref, mc_ref):
        vxt = vxt_ref[0:V_ROWS, pl.ds(offset(c, CK), CK)]
        m_old = m_ref[h]
        m_new = jnp.maximum(m_old, mc_ref[h])
        alpha = jnp.exp2(m_old - m_new)
        p = jnp.exp2(s_ref[h] - m_new[0:1, :])
        acc_ref[h, 0:V_ROWS, :] = acc_ref[h, 0:V_ROWS, :] * alpha[0:1, :] + jnp.dot(
            vxt, p, preferred_element_type=jnp.float32)
        m_ref[h] = m_new

    def step(qb, c, parity, qb_next, c_next):
        for h in range(n_heads):
            scores(qb_next, c_next, h, *bufs[1 - parity])
            accumulate(c, h, *bufs[parity])

    def finalize(qb):
        lane = jax.lax.broadcasted_iota(jnp.int32, (TQ, LANES), 1)
        for j in range(n_heads // 2):
            halves = []
            for h in (2 * j, 2 * j + 1):
                acc = acc_ref[h]
                halves.append((acc * (1.0 / acc[HEAD_DIM:HEAD_DIM + 1, :])).T)
            pair = jnp.where(lane < HEAD_DIM, halves[0], pltpu.roll(halves[1], HEAD_DIM, 1))
            rows = pl.ds(offset(qb, TQ), TQ)
            gate = ga_ref[rows, j * LANES:(j + 1) * LANES].astype(jnp.float32)
            o_ref[rows, j * LANES:(j + 1) * LANES] = (pair * gate).astype(o_ref.dtype)
        m_ref[...] = jnp.full(m_ref.shape, NEG_BIG, jnp.float32)

    m_ref[...] = jnp.full(m_ref.shape, NEG_BIG, jnp.float32)
    acc_ref[...] = jnp.zeros(acc_ref.shape, jnp.float32)
    for h in range(n_heads):
        scores(0, 0, h, *bufs[0])

    n_loop = (n_chunks - 1) // STEPS_PER_BODY

    def qblock(qb, carry):
        def body(i, carry):
            for u in range(STEPS_PER_BODY):
                c = STEPS_PER_BODY * i + u
                step(qb, c, u % 2, qb, c + 1)
            return carry

        jax.lax.fori_loop(0, n_loop, body, 0)
        for c in range(n_loop * STEPS_PER_BODY, n_chunks - 1):
            step(qb, c, c % 2, qb, c + 1)
        step(qb, n_chunks - 1, 1, jnp.minimum(qb + 1, n_qblocks - 1), 0)
        finalize(qb)
        return carry

    jax.lax.fori_loop(0, n_qblocks, qblock, 0)


def _post_kernel(attn_ref, rest_ref, x_ref, wpa_ref, wpg_ref, wout_ref, ws_ref,
                 bsp_ref, fg_ref, o_ref, vmix_ref):
    f32 = jnp.float32
    tm = x_ref.shape[0]

    def rest(lo, hi):
        return rest_ref[:, lo:hi].astype(f32)

    y_a = jnp.dot(attn_ref[...], wpa_ref[...], preferred_element_type=f32)

    o0 = 0
    vv = rest_ref[:, o0 + GMLP_WIDTH:o0 + 2 * GMLP_WIDTH]
    low = jax.lax.broadcasted_iota(jnp.int32, (CHUNK, LANES), 1) < GROUP_DIM
    zero = jnp.zeros((CHUNK, LANES), jnp.bfloat16)
    for c in range(tm // CHUNK):
        slabs = []
        for j in range(GMLP_WIDTH // LANES):
            v_slab = vv[c * CHUNK:(c + 1) * CHUNK, j * LANES:(j + 1) * LANES]
            rhs = jnp.concatenate([jnp.where(low, v_slab, zero), jnp.where(low, zero, v_slab)],
                                  axis=0)
            slabs.append(jnp.dot(ws_ref[j], rhs, preferred_element_type=f32))
        vmix_ref[c * CHUNK:(c + 1) * CHUNK, :] = jnp.concatenate(slabs, axis=1) + bsp_ref[...]
    gm = rest(o0, o0 + GMLP_WIDTH) * vmix_ref[...] * rest(o0 + 2 * GMLP_WIDTH, o0 + 3 * GMLP_WIDTH)
    y_b = jnp.dot(gm, wpg_ref[...], preferred_element_type=f32)

    o1 = o0 + 3 * GMLP_WIDTH
    y = rest(o1, o1 + D_MODEL) * y_a + rest(o1 + D_MODEL, o1 + 2 * D_MODEL) * y_b
    out = x_ref[...] + jnp.dot(y, wout_ref[...], preferred_element_type=f32)
    ms = jnp.mean(out * out, axis=-1, keepdims=True)
    o_ref[...] = out * jax.lax.rsqrt(ms + EPS) * fg_ref[...]


def _rope_tables(seq):
    token = np.arange(seq)
    inv_freq = ROPE_THETA ** (-np.arange(0, AXIAL_DIM, 2, dtype=np.float64) / AXIAL_DIM)
    row_ang = (token // GRID_W)[:, None] * inv_freq
    col_ang = (token % GRID_W)[:, None] * inv_freq
    cos = np.concatenate([np.cos(row_ang)] * 2 + [np.cos(col_ang)] * 2, axis=-1)
    sin = np.concatenate([-np.sin(row_ang), np.sin(row_ang),
                          -np.sin(col_ang), np.sin(col_ang)], axis=-1)
    return (jnp.asarray(np.tile(cos, (1, 2)), jnp.float32),
            jnp.asarray(np.tile(sin, (1, 2)), jnp.float32))


def _layer(x2, batch, seq, norm_gain, w_in, q_gain, k_gain, w_proj_attn, ln_v_gain, ln_v_bias,
           w_spatial, b_spatial, w_proj_gmlp, b_merge, w_out, final_gain):
    n = batch * seq
    bf16 = jnp.bfloat16
    f32 = jnp.float32
    cos, sin = _rope_tables(seq)
    seg = np.arange(2 * LANES) // HEAD_DIM
    block_diag = seg[:, None] == seg[None, :]
    ones_bd = jnp.asarray(np.concatenate([block_diag, block_diag], axis=0), dtype=bf16)
    row = lambda v: v.reshape(1, -1).astype(f32)
    const = lambda shape: pl.BlockSpec(shape, lambda *_: (0,) * len(shape))
    in_width = w_in.shape[1]
    seq_blocks = seq // TM_IN

    qt, kx, vxt, gate_a, rest = pl.pallas_call(
        _in_proj_kernel,
        grid=(n // TM_IN,),
        in_specs=[
            pl.BlockSpec((TM_IN, D_MODEL), lambda i: (i, 0)),
            const((1, D_MODEL)),
            pl.BlockSpec((D_MODEL, in_width), lambda i: (0, 0), pipeline_mode=pl.Buffered(1)),
            const((1, LANES)),
            const((1, LANES)),
            pl.BlockSpec((TM_IN, LANES), lambda i: (i % seq_blocks, 0)),
            pl.BlockSpec((TM_IN, LANES), lambda i: (i % seq_blocks, 0)),
            const((4 * LANES, 2 * LANES)),
            const((1, GMLP_WIDTH)),
            const((1, GMLP_WIDTH)),
            const((1, N_BRANCHES * D_MODEL)),
        ],
        out_specs=[
            pl.BlockSpec((ATTN_WIDTH, TM_IN), lambda i: (0, i)),
            pl.BlockSpec((TM_IN, 4 * LANES), lambda i: (i, 0)),
            pl.BlockSpec((2 * LANES, TM_IN), lambda i: (0, i)),
            pl.BlockSpec((TM_IN, ATTN_WIDTH), lambda i: (i, 0)),
            pl.BlockSpec((TM_IN, REST_WIDTH), lambda i: (i, 0)),
        ],
        out_shape=[
            jax.ShapeDtypeStruct((ATTN_WIDTH, n), bf16),
            jax.ShapeDtypeStruct((n, 4 * LANES), bf16),
            jax.ShapeDtypeStruct((2 * LANES, n), f32),
            jax.ShapeDtypeStruct((n, ATTN_WIDTH), bf16),
            jax.ShapeDtypeStruct((n, REST_WIDTH), bf16),
        ],
        compiler_params=pltpu.CompilerParams(
            dimension_semantics=("arbitrary",), vmem_limit_bytes=VMEM_LIMIT_BYTES),
        name="in_proj",
    )(x2, row(norm_gain), w_in, row(jnp.tile(q_gain, 2)), row(jnp.tile(k_gain, 2)),
      cos, sin, ones_bd, row(ln_v_gain), row(ln_v_bias), row(b_merge))

    kv_width = GQA_GROUP * HEAD_DIM
    attn = pl.pallas_call(
        _attn_kernel,
        grid=(batch, N_KV_HEADS),
        in_specs=[
            pl.BlockSpec((kv_width, seq), lambda b, g: (g, b)),
            pl.BlockSpec((seq, 2 * LANES), lambda b, g: (b, g)),
            pl.BlockSpec((LANES, seq), lambda b, g: (g, b)),
            pl.BlockSpec((seq, kv_width), lambda b, g: (b, g)),
        ],
        out_specs=pl.BlockSpec((seq, kv_width), lambda b, g: (b, g)),
        out_shape=jax.ShapeDtypeStruct((n, ATTN_WIDTH), f32),
        scratch_shapes=[
            pltpu.VMEM((GQA_GROUP, 8, TQ), f32),
            pltpu.VMEM((GQA_GROUP, LANES, TQ), f32),
            pltpu.VMEM((GQA_GROUP, CK, TQ), f32),
            pltpu.VMEM((GQA_GROUP, CK, TQ), f32),
            pltpu.VMEM((GQA_GROUP, 8, TQ), f32),
            pltpu.VMEM((GQA_GROUP, 8, TQ), f32),
        ],
        compiler_params=pltpu.CompilerParams(
            dimension_semantics=("arbitrary", "arbitrary"),
            vmem_limit_bytes=VMEM_LIMIT_BYTES),
        name="attention",
    )(qt, kx, vxt, gate_a)

    ws_stack = w_spatial.reshape(N_GROUPS // 2, 2, CHUNK, CHUNK).transpose(0, 2, 1, 3).reshape(
        N_GROUPS // 2, CHUNK, 2 * CHUNK).astype(bf16)
    b_sp = jnp.repeat(b_spatial.T, GROUP_DIM, axis=1).astype(f32)
    out = pl.pallas_call(
        _post_kernel,
        grid=(n // TM_POST,),
        in_specs=[
            pl.BlockSpec((TM_POST, ATTN_WIDTH), lambda i: (i, 0)),
            pl.BlockSpec((TM_POST, REST_WIDTH), lambda i: (i, 0)),
            pl.BlockSpec((TM_POST, D_MODEL), lambda i: (i, 0)),
            const((ATTN_WIDTH, D_MODEL)),
            const((GMLP_WIDTH, D_MODEL)),
            const((D_MODEL, D_MODEL)),
            const((N_GROUPS // 2, CHUNK, 2 * CHUNK)),
            const((CHUNK, GMLP_WIDTH)),
            const((1, D_MODEL)),
        ],
        out_specs=pl.BlockSpec((TM_POST, D_MODEL), lambda i: (i, 0)),
        out_shape=jax.ShapeDtypeStruct((n, D_MODEL), f32),
        scratch_shapes=[pltpu.VMEM((TM_POST, GMLP_WIDTH), f32)],
        compiler_params=pltpu.CompilerParams(
            dimension_semantics=("arbitrary",), vmem_limit_bytes=VMEM_LIMIT_BYTES),
        name="post",
    )(attn, rest, x2, w_proj_attn, w_proj_gmlp, w_out,
      ws_stack, b_sp, row(final_gain))
    return out


def kernel(x, norm_gain, w_in, q_gain, k_gain, w_proj_attn, ln_v_gain, ln_v_bias, w_spatial,
           b_spatial, w_proj_gmlp, b_merge, w_out, final_gain):
    batch, seq, d = x.shape
    depth = w_in.shape[0]
    assert depth == 1, "the final rmsnorm is fused into the single layer's last call"
    assert d == D_MODEL and seq % TM_IN == 0 and seq % TQ == 0 and seq % CK == 0
    out = _layer(x.reshape(batch * seq, d), batch, seq, norm_gain[0], w_in[0], q_gain[0],
                 k_gain[0], w_proj_attn[0], ln_v_gain[0], ln_v_bias[0], w_spatial[0],
                 b_spatial[0], w_proj_gmlp[0], b_merge[0], w_out[0], final_gain)
    return out.reshape(batch, seq, d)
```

```python
import functools
import math

import jax
import jax.numpy as jnp
import numpy as np
from jax.experimental import pallas as pl
from jax.experimental.pallas import tpu as pltpu

D_MODEL = 1024
GRID_W = 64
HEAD_DIM = 64
N_Q_HEADS = 8
N_KV_HEADS = 2
GQA_GROUP = N_Q_HEADS // N_KV_HEADS
ATTN_WIDTH = N_Q_HEADS * HEAD_DIM
KV_WIDTH = N_KV_HEADS * HEAD_DIM
AXIAL_DIM = HEAD_DIM // 2
ROPE_THETA = 10000.0
CHUNK = 128
N_GROUPS = 8
GMLP_WIDTH = 512
GROUP_DIM = GMLP_WIDTH // N_GROUPS
N_BRANCHES = 2
EPS = 1e-6
IN_REST_WIDTH = ATTN_WIDTH + 3 * GMLP_WIDTH + N_BRANCHES * D_MODEL
REST_WIDTH = IN_REST_WIDTH - GMLP_WIDTH

LANES = 128
VMEM_LIMIT_BYTES = 56 * 1024 * 1024

TM_IN = 512
REST_STEP = 512
QKV_AFTER_CHUNKS = 5
SEGSUM_AFTER_CHUNKS = 7
TQ = 256
CK = 512
REGION_STEPS = (12, 4)
V_ROWS = 72
TM_POST = 1024
NEG_BIG = -1e30


def _segment_sums(z, bd_stack):
    sq = z * z
    hi = sq.astype(jnp.bfloat16)
    lo = (sq - hi.astype(jnp.float32)).astype(jnp.bfloat16)
    return jnp.dot(jnp.concatenate([hi, lo], axis=1), bd_stack,
                   preferred_element_type=jnp.float32)


def _head_norm_rope(z, sums, gain, cos, sin_signed, scale):
    ms = sums * (1.0 / HEAD_DIM)
    y = z * (jax.lax.rsqrt(ms + EPS) * scale) * gain
    lane = jax.lax.broadcasted_iota(jnp.int32, y.shape, 1)
    half = AXIAL_DIM // 2
    partner = jnp.where((lane & half) != 0,
                        pltpu.roll(y, half, 1),
                        pltpu.roll(y, LANES - half, 1))
    return y * cos + partner * sin_signed


def _in_proj_kernel(x_ref, ng_ref, w_ref, qg_ref, kg_ref, cos_ref, sin_ref, bd_ref,
                    lng_ref, lnb_ref, bm_ref, qt_ref, kx_ref, vxt_ref, rest_ref):
    f32 = jnp.float32
    x = x_ref[...]
    ms = jnp.mean(x * x, axis=-1, keepdims=True)
    h = x * jax.lax.rsqrt(ms + EPS) * ng_ref[...]

    def project(lo, hi):
        return jnp.dot(h, w_ref[:, lo:hi], preferred_element_type=f32)

    def layernorm_v(z):
        g = jax.nn.gelu(z)
        mu = jnp.mean(g, axis=-1, keepdims=True)
        var = jnp.mean(jnp.square(g - mu), axis=-1, keepdims=True)
        return (g - mu) * jax.lax.rsqrt(var + EPS) * lng_ref[...] + lnb_ref[...]

    def merge_gate(c):
        lo = (c - 4) * REST_STEP
        return lambda z: jax.nn.sigmoid(z + bm_ref[:, lo:lo + REST_STEP])

    n_rest = IN_REST_WIDTH // REST_STEP
    activations = [jax.nn.silu, jax.nn.gelu, layernorm_v, jax.nn.silu] + [
        merge_gate(c) for c in range(4, n_rest)]
    rest_column = [0, None, 2, 1] + list(range(3, n_rest - 1))
    held = {}

    def rest_chunks(lo, hi):
        base = ATTN_WIDTH + 2 * KV_WIDTH
        for c in range(lo, hi):
            y = activations[c](project(base + c * REST_STEP, base + (c + 1) * REST_STEP))
            if c == 1:
                held["u"] = y
                continue
            if c == 3:
                y = held["u"] * y
            col = rest_column[c] * REST_STEP
            rest_ref[:, col:col + REST_STEP] = y.astype(rest_ref.dtype)

    rest_chunks(0, QKV_AFTER_CHUNKS)
    zq = project(0, ATTN_WIDTH)
    zkv = project(ATTN_WIDTH, ATTN_WIDTH + 2 * KV_WIDTH)
    rest_chunks(QKV_AFTER_CHUNKS, SEGSUM_AFTER_CHUNKS)
    bd_stack = bd_ref[...]
    sums_q = [_segment_sums(zq[:, i * 2 * LANES:(i + 1) * 2 * LANES], bd_stack)
              for i in range(ATTN_WIDTH // (2 * LANES))]
    sums_kv = _segment_sums(zkv, bd_stack)
    rest_chunks(SEGSUM_AFTER_CHUNKS, n_rest)

    cos = cos_ref[...]
    sin_signed = sin_ref[...]
    lane = jax.lax.broadcasted_iota(jnp.int32, (x.shape[0], LANES), 1)
    low = lane < HEAD_DIM

    scale = math.log2(math.e) / math.sqrt(HEAD_DIM)
    for j in range(ATTN_WIDTH // LANES):
        sums = sums_q[j // 2][:, (j % 2) * LANES:(j % 2 + 1) * LANES]
        q = _head_norm_rope(zq[:, j * LANES:(j + 1) * LANES], sums, qg_ref[...],
                            cos, sin_signed, scale)
        qt_ref[j * LANES:(j + 1) * LANES, :] = q.T.astype(qt_ref.dtype)

    k = _head_norm_rope(zkv[:, 0:LANES], sums_kv[:, 0:LANES], kg_ref[...], cos, sin_signed, 1.0)
    k_sw = pltpu.roll(k, HEAD_DIM, 1)
    zero = jnp.zeros_like(k)
    kx_ref[:, 0 * LANES:1 * LANES] = jnp.where(low, k, zero).astype(kx_ref.dtype)
    kx_ref[:, 1 * LANES:2 * LANES] = jnp.where(low, zero, k_sw).astype(kx_ref.dtype)
    kx_ref[:, 2 * LANES:3 * LANES] = jnp.where(low, k_sw, zero).astype(kx_ref.dtype)
    kx_ref[:, 3 * LANES:4 * LANES] = jnp.where(low, zero, k).astype(kx_ref.dtype)

    v = zkv[:, LANES:2 * LANES]
    v_sw = pltpu.roll(v, HEAD_DIM, 1)
    ones_col = jnp.where(lane == HEAD_DIM, 1.0, 0.0)
    vxt_ref[0 * LANES:1 * LANES, :] = jnp.where(low, v, ones_col).T.astype(vxt_ref.dtype)
    vxt_ref[1 * LANES:2 * LANES, :] = jnp.where(low, v_sw, ones_col).T.astype(vxt_ref.dtype)


def _attn_kernel(qt_ref, kx_ref, vxt_ref, o_ref, m_ref, acc_ref, s0_ref, s1_ref, mc0_ref, mc1_ref):
    n_heads = GQA_GROUP
    seq = kx_ref.shape[0]
    n_chunks = seq // CK
    n_qblocks = seq // TQ
    bufs = ((s0_ref, mc0_ref), (s1_ref, mc1_ref))
    assert n_chunks % 2 == 0 and sum(REGION_STEPS) == n_chunks

    def offset(i, size):
        return i * size if isinstance(i, int) else pl.multiple_of(i * size, size)

    def scores(qb, c, h, s_ref, mc_ref):
        qt = qt_ref[(h // 2) * LANES:(h // 2 + 1) * LANES, pl.ds(offset(qb, TQ), TQ)]
        kv = kx_ref[pl.ds(offset(c, CK), CK), (h % 2) * LANES:(h % 2 + 1) * LANES]
        s = jnp.dot(kv, qt, preferred_element_type=jnp.float32)
        s_ref[h] = s
        mc_ref[h] = jnp.broadcast_to(jnp.max(s, axis=0, keepdims=True), mc_ref.shape[1:])

    def accumulate(slot, c, h, s_ref, mc_ref):
        vxt = vxt_ref[0:V_ROWS, pl.ds(offset(c, CK), CK)]
        m_old = m_ref[slot, h]
        m_new = jnp.maximum(m_old, mc_ref[h])
        alpha = jnp.exp2(m_old - m_new)
        p = jnp.exp2(s_ref[h] - m_new[0:1, :])
        acc_ref[slot, h, 0:V_ROWS, :] = acc_ref[slot, h, 0:V_ROWS, :] * alpha[0:1, :] + jnp.dot(
            vxt, p, preferred_element_type=jnp.float32)
        m_ref[slot, h] = m_new

    def step(slot, qb, c, parity, qb_next, c_next):
        for h in range(n_heads):
            scores(qb_next, c_next, h, *bufs[1 - parity])
            accumulate(slot, c, h, *bufs[parity])

    def finalize(slot, qb):
        lane = jax.lax.broadcasted_iota(jnp.int32, (TQ, LANES), 1)
        for j in range(n_heads // 2):
            halves = []
            for h in (2 * j, 2 * j + 1):
                acc = acc_ref[slot, h]
                halves.append((acc * (1.0 / acc[HEAD_DIM:HEAD_DIM + 1, :])).T)
            pair = jnp.where(lane < HEAD_DIM, halves[0], pltpu.roll(halves[1], HEAD_DIM, 1))
            o_ref[pl.ds(offset(qb, TQ), TQ), j * LANES:(j + 1) * LANES] = pair.astype(o_ref.dtype)

    acc_ref[...] = jnp.ones(acc_ref.shape, jnp.float32)
    for h in range(n_heads):
        scores(0, 0, h, *bufs[0])

    def qblock(qb, carry):
        slot = qb % 2
        m_ref[slot] = jnp.full(m_ref.shape[1:], NEG_BIG, jnp.float32)

        one = jnp.minimum(qb, 0) + 1
        first = 0
        for size in REGION_STEPS:
            chunks = range(first, first + size)
            first += size

            def region(_, carry, chunks=chunks):
                for c in chunks:
                    if c == n_chunks - REGION_STEPS[-1]:
                        finalize(1 - slot, jnp.maximum(qb - 1, 0))
                    if c < n_chunks - 1:
                        step(slot, qb, c, c % 2, qb, c + 1)
                    else:
                        step(slot, qb, c, c % 2, jnp.minimum(qb + 1, n_qblocks - 1), 0)
                return carry

            jax.lax.fori_loop(0, one, region, 0)
        return carry

    jax.lax.fori_loop(0, n_qblocks, qblock, 0)
    finalize((n_qblocks - 1) % 2, n_qblocks - 1)


def _post_kernel(attn_ref, rest_ref, x_ref, wpa_ref, wpg_ref, wout_ref, ws_ref,
                 bsp_ref, fg_ref, o_ref, vmix_ref):
    f32 = jnp.float32
    tm = x_ref.shape[0]

    def rest(lo, hi):
        return rest_ref[:, lo:hi].astype(f32)

    a = attn_ref[...].astype(f32) * rest(0, ATTN_WIDTH)
    y_a = jnp.dot(a, wpa_ref[...], preferred_element_type=f32)

    o0 = ATTN_WIDTH
    vv = rest_ref[:, o0 + GMLP_WIDTH:o0 + 2 * GMLP_WIDTH]
    low = jax.lax.broadcasted_iota(jnp.int32, (CHUNK, LANES), 1) < GROUP_DIM
    zero = jnp.zeros((CHUNK, LANES), jnp.bfloat16)
    for c in range(tm // CHUNK):
        slabs = []
        for j in range(GMLP_WIDTH // LANES):
            v_slab = vv[c * CHUNK:(c + 1) * CHUNK, j * LANES:(j + 1) * LANES]
            rhs = jnp.concatenate([jnp.where(low, v_slab, zero), jnp.where(low, zero, v_slab)],
                                  axis=0)
            slabs.append(jnp.dot(ws_ref[j], rhs, preferred_element_type=f32))
        vmix_ref[c * CHUNK:(c + 1) * CHUNK, :] = jnp.concatenate(slabs, axis=1) + bsp_ref[...]
    gm = rest(o0, o0 + GMLP_WIDTH) * vmix_ref[...]
    y_b = jnp.dot(gm, wpg_ref[...], preferred_element_type=f32)

    o1 = o0 + 2 * GMLP_WIDTH
    y = rest(o1, o1 + D_MODEL) * y_a + rest(o1 + D_MODEL, o1 + 2 * D_MODEL) * y_b
    out = x_ref[...] + jnp.dot(y, wout_ref[...], preferred_element_type=f32)
    ms = jnp.mean(out * out, axis=-1, keepdims=True)
    o_ref[...] = out * jax.lax.rsqrt(ms + EPS) * fg_ref[...]


def _rope_tables(seq):
    token = np.arange(seq)
    inv_freq = ROPE_THETA ** (-np.arange(0, AXIAL_DIM, 2, dtype=np.float64) / AXIAL_DIM)
    row_ang = (token // GRID_W)[:, None] * inv_freq
    col_ang = (token % GRID_W)[:, None] * inv_freq
    cos = np.concatenate([np.cos(row_ang)] * 2 + [np.cos(col_ang)] * 2, axis=-1)
    sin = np.concatenate([-np.sin(row_ang), np.sin(row_ang),
                          -np.sin(col_ang), np.sin(col_ang)], axis=-1)
    return (jnp.asarray(np.tile(cos, (1, 2)), jnp.float32),
            jnp.asarray(np.tile(sin, (1, 2)), jnp.float32))


def _layer(x2, batch, seq, norm_gain, w_in, q_gain, k_gain, w_proj_attn, ln_v_gain, ln_v_bias,
           w_spatial, b_spatial, w_proj_gmlp, b_merge, w_out, final_gain):
    n = batch * seq
    bf16 = jnp.bfloat16
    f32 = jnp.float32
    cos, sin = _rope_tables(seq)
    seg = np.arange(2 * LANES) // HEAD_DIM
    block_diag = seg[:, None] == seg[None, :]
    ones_bd = jnp.asarray(np.concatenate([block_diag, block_diag], axis=0), dtype=bf16)
    row = lambda v: v.reshape(1, -1).astype(f32)
    const = lambda shape: pl.BlockSpec(shape, lambda *_: (0,) * len(shape))
    in_width = w_in.shape[1]
    seq_blocks = seq // TM_IN

    qt, kx, vxt, rest = pl.pallas_call(
        _in_proj_kernel,
        grid=(n // TM_IN,),
        in_specs=[
            pl.BlockSpec((TM_IN, D_MODEL), lambda i: (i, 0)),
            const((1, D_MODEL)),
            pl.BlockSpec((D_MODEL, in_width), lambda i: (0, 0), pipeline_mode=pl.Buffered(1)),
            const((1, LANES)),
            const((1, LANES)),
            pl.BlockSpec((TM_IN, LANES), lambda i: (i % seq_blocks, 0)),
            pl.BlockSpec((TM_IN, LANES), lambda i: (i % seq_blocks, 0)),
            const((4 * LANES, 2 * LANES)),
            const((1, GMLP_WIDTH)),
            const((1, GMLP_WIDTH)),
            const((1, N_BRANCHES * D_MODEL)),
        ],
        out_specs=[
            pl.BlockSpec((ATTN_WIDTH, TM_IN), lambda i: (0, i)),
            pl.BlockSpec((TM_IN, 4 * LANES), lambda i: (i, 0)),
            pl.BlockSpec((2 * LANES, TM_IN), lambda i: (0, i)),
            pl.BlockSpec((TM_IN, REST_WIDTH), lambda i: (i, 0)),
        ],
        out_shape=[
            jax.ShapeDtypeStruct((ATTN_WIDTH, n), bf16),
            jax.ShapeDtypeStruct((n, 4 * LANES), bf16),
            jax.ShapeDtypeStruct((2 * LANES, n), f32),
            jax.ShapeDtypeStruct((n, REST_WIDTH), bf16),
        ],
        compiler_params=pltpu.CompilerParams(
            dimension_semantics=("arbitrary",), vmem_limit_bytes=VMEM_LIMIT_BYTES),
        name="in_proj",
    )(x2, row(norm_gain), w_in, row(jnp.tile(q_gain, 2)), row(jnp.tile(k_gain, 2)),
      cos, sin, ones_bd, row(ln_v_gain), row(ln_v_bias), row(b_merge))

    kv_width = GQA_GROUP * HEAD_DIM
    attn = pl.pallas_call(
        _attn_kernel,
        grid=(batch, N_KV_HEADS),
        in_specs=[
            pl.BlockSpec((kv_width, seq), lambda b, g: (g, b)),
            pl.BlockSpec((seq, 2 * LANES), lambda b, g: (b, g)),
            pl.BlockSpec((LANES, seq), lambda b, g: (g, b)),
        ],
        out_specs=pl.BlockSpec((seq, kv_width), lambda b, g: (b, g)),
        out_shape=jax.ShapeDtypeStruct((n, ATTN_WIDTH), bf16),
        scratch_shapes=[
            pltpu.VMEM((2, GQA_GROUP, 8, TQ), f32),
            pltpu.VMEM((2, GQA_GROUP, LANES, TQ), f32),
            pltpu.VMEM((GQA_GROUP, CK, TQ), f32),
            pltpu.VMEM((GQA_GROUP, CK, TQ), f32),
            pltpu.VMEM((GQA_GROUP, 8, TQ), f32),
            pltpu.VMEM((GQA_GROUP, 8, TQ), f32),
        ],
        compiler_params=pltpu.CompilerParams(
            dimension_semantics=("arbitrary", "arbitrary"),
            vmem_limit_bytes=VMEM_LIMIT_BYTES),
        name="attention",
    )(qt, kx, vxt)

    ws_stack = w_spatial.reshape(N_GROUPS // 2, 2, CHUNK, CHUNK).transpose(0, 2, 1, 3).reshape(
        N_GROUPS // 2, CHUNK, 2 * CHUNK).astype(bf16)
    b_sp = jnp.repeat(b_spatial.T, GROUP_DIM, axis=1).astype(f32)
    out = pl.pallas_call(
        _post_kernel,
        grid=(n // TM_POST,),
        in_specs=[
            pl.BlockSpec((TM_POST, ATTN_WIDTH), lambda i: (i, 0)),
            pl.BlockSpec((TM_POST, REST_WIDTH), lambda i: (i, 0)),
            pl.BlockSpec((TM_POST, D_MODEL), lambda i: (i, 0)),
            const((ATTN_WIDTH, D_MODEL)),
            const((GMLP_WIDTH, D_MODEL)),
            const((D_MODEL, D_MODEL)),
            const((N_GROUPS // 2, CHUNK, 2 * CHUNK)),
            const((CHUNK, GMLP_WIDTH)),
            const((1, D_MODEL)),
        ],
        out_specs=pl.BlockSpec((TM_POST, D_MODEL), lambda i: (i, 0)),
        out_shape=jax.ShapeDtypeStruct((n, D_MODEL), f32),
        scratch_shapes=[pltpu.VMEM((TM_POST, GMLP_WIDTH), f32)],
        compiler_params=pltpu.CompilerParams(
            dimension_semantics=("arbitrary",), vmem_limit_bytes=VMEM_LIMIT_BYTES),
        name="post",
    )(attn, rest, x2, w_proj_attn, w_proj_gmlp, w_out,
      ws_stack, b_sp, row(final_gain))
    return out


def kernel(x, norm_gain, w_in, q_gain, k_gain, w_proj_attn, ln_v_gain, ln_v_bias, w_spatial,
           b_spatial, w_proj_gmlp, b_merge, w_out, final_gain):
    batch, seq, d = x.shape
    depth = w_in.shape[0]
    assert depth == 1, "the final rmsnorm is fused into the single layer's last call"
    assert d == D_MODEL and seq % TM_IN == 0 and seq % TQ == 0 and seq % CK == 0
    out = _layer(x.reshape(batch * seq, d), batch, seq, norm_gain[0], w_in[0], q_gain[0],
                 k_gain[0], w_proj_attn[0], ln_v_gain[0], ln_v_bias[0], w_spatial[0],
                 b_spatial[0], w_proj_gmlp[0], b_merge[0], w_out[0], final_gain)
    return out.reshape(batch, seq, d)
```

```python
import math

import jax
import jax.numpy as jnp
import numpy as np
from jax.experimental import pallas as pl
from jax.experimental.pallas import tpu as pltpu

D_MODEL = 1024
GRID_W = 64
HEAD_DIM = 64
N_Q_HEADS = 8
N_KV_HEADS = 2
GQA_GROUP = N_Q_HEADS // N_KV_HEADS
ATTN_WIDTH = N_Q_HEADS * HEAD_DIM
KV_WIDTH = N_KV_HEADS * HEAD_DIM
AXIAL_DIM = HEAD_DIM // 2
ROPE_THETA = 10000.0
CHUNK = 128
N_GROUPS = 8
GMLP_WIDTH = 512
GROUP_DIM = GMLP_WIDTH // N_GROUPS
N_BRANCHES = 2
EPS = 1e-6
IN_REST_WIDTH = ATTN_WIDTH + 3 * GMLP_WIDTH + N_BRANCHES * D_MODEL
REST_WIDTH = IN_REST_WIDTH - GMLP_WIDTH

LANES = 128
MIB = 1024 * 1024
VMEM_LIMIT_IN_PROJ = 44 * MIB
VMEM_LIMIT_ATTENTION = 42 * MIB
VMEM_LIMIT_POST = 36 * MIB

TM_IN = 512
REST_STEP = 512
QKV_AFTER_CHUNKS = 5
SEGSUM_AFTER_CHUNKS = 7
TQ = 256
CK = 512
REGION_STEPS = (12, 4)
V_ROWS = 72
TM_POST = 512
NEG_BIG = -1e30


def _segment_sums(z, bd_stack):
    sq = z * z
    hi = sq.astype(jnp.bfloat16)
    lo = (sq - hi.astype(jnp.float32)).astype(jnp.bfloat16)
    return jnp.dot(jnp.concatenate([hi, lo], axis=1), bd_stack,
                   preferred_element_type=jnp.float32)


def _head_norm_rope(z, sums, gain, cos, sin_signed, scale):
    ms = sums * (1.0 / HEAD_DIM)
    y = z * (jax.lax.rsqrt(ms + EPS) * scale) * gain
    lane = jax.lax.broadcasted_iota(jnp.int32, y.shape, 1)
    half = AXIAL_DIM // 2
    partner = jnp.where((lane & half) != 0,
                        pltpu.roll(y, half, 1),
                        pltpu.roll(y, LANES - half, 1))
    return y * cos + partner * sin_signed


def _in_proj_kernel(x_ref, ng_ref, w_ref, qg_ref, kg_ref, cos_ref, sin_ref, bd_ref,
                    lng_ref, lnb_ref, bm_ref, qt_ref, kx_ref, vxt_ref, rest_ref):
    f32 = jnp.float32
    x = x_ref[...]
    ms = jnp.mean(x * x, axis=-1, keepdims=True)
    h = x * jax.lax.rsqrt(ms + EPS) * ng_ref[...]

    def project(lo, hi):
        return jnp.dot(h, w_ref[:, lo:hi], preferred_element_type=f32)

    def layernorm_v(z):
        g = jax.nn.gelu(z)
        mu = jnp.mean(g, axis=-1, keepdims=True)
        var = jnp.mean(jnp.square(g - mu), axis=-1, keepdims=True)
        return (g - mu) * jax.lax.rsqrt(var + EPS) * lng_ref[...] + lnb_ref[...]

    def merge_gate(c):
        lo = (c - 4) * REST_STEP
        return lambda z: jax.nn.sigmoid(z + bm_ref[:, lo:lo + REST_STEP])

    n_rest = IN_REST_WIDTH // REST_STEP
    activations = [jax.nn.silu, jax.nn.gelu, layernorm_v, jax.nn.silu] + [
        merge_gate(c) for c in range(4, n_rest)]
    rest_column = [0, None, 2, 1] + list(range(3, n_rest - 1))
    held = {}

    def rest_chunks(lo, hi):
        base = ATTN_WIDTH + 2 * KV_WIDTH
        for c in range(lo, hi):
            y = activations[c](project(base + c * REST_STEP, base + (c + 1) * REST_STEP))
            if c == 1:
                held["u"] = y
                continue
            if c == 3:
                y = held["u"] * y
            col = rest_column[c] * REST_STEP
            rest_ref[:, col:col + REST_STEP] = y.astype(rest_ref.dtype)

    rest_chunks(0, QKV_AFTER_CHUNKS)
    zq = project(0, ATTN_WIDTH)
    zkv = project(ATTN_WIDTH, ATTN_WIDTH + 2 * KV_WIDTH)
    rest_chunks(QKV_AFTER_CHUNKS, SEGSUM_AFTER_CHUNKS)
    bd_stack = bd_ref[...]
    sums_q = [_segment_sums(zq[:, i * 2 * LANES:(i + 1) * 2 * LANES], bd_stack)
              for i in range(ATTN_WIDTH // (2 * LANES))]
    sums_kv = _segment_sums(zkv, bd_stack)
    rest_chunks(SEGSUM_AFTER_CHUNKS, n_rest)

    cos = cos_ref[...]
    sin_signed = sin_ref[...]
    lane = jax.lax.broadcasted_iota(jnp.int32, (x.shape[0], LANES), 1)
    low = lane < HEAD_DIM
    q_gain = jnp.concatenate([qg_ref[...]] * (LANES // HEAD_DIM), axis=1)
    k_gain = jnp.concatenate([kg_ref[...]] * (LANES // HEAD_DIM), axis=1)

    scale = math.log2(math.e) / math.sqrt(HEAD_DIM)
    for j in range(ATTN_WIDTH // LANES):
        sums = sums_q[j // 2][:, (j % 2) * LANES:(j % 2 + 1) * LANES]
        q = _head_norm_rope(zq[:, j * LANES:(j + 1) * LANES], sums, q_gain, cos, sin_signed, scale)
        qt_ref[j * LANES:(j + 1) * LANES, :] = q.T.astype(qt_ref.dtype)

    k = _head_norm_rope(zkv[:, 0:LANES], sums_kv[:, 0:LANES], k_gain, cos, sin_signed, 1.0)
    k_sw = pltpu.roll(k, HEAD_DIM, 1)
    zero = jnp.zeros_like(k)
    kx_ref[:, 0 * LANES:1 * LANES] = jnp.where(low, k, zero).astype(kx_ref.dtype)
    kx_ref[:, 1 * LANES:2 * LANES] = jnp.where(low, zero, k_sw).astype(kx_ref.dtype)
    kx_ref[:, 2 * LANES:3 * LANES] = jnp.where(low, k_sw, zero).astype(kx_ref.dtype)
    kx_ref[:, 3 * LANES:4 * LANES] = jnp.where(low, zero, k).astype(kx_ref.dtype)

    v = zkv[:, LANES:2 * LANES]
    v_sw = pltpu.roll(v, HEAD_DIM, 1)
    ones_col = jnp.where(lane == HEAD_DIM, 1.0, 0.0)
    vxt_ref[0 * LANES:1 * LANES, :] = jnp.where(low, v, ones_col).T.astype(vxt_ref.dtype)
    vxt_ref[1 * LANES:2 * LANES, :] = jnp.where(low, v_sw, ones_col).T.astype(vxt_ref.dtype)


def _attn_kernel(qt_ref, kx_ref, vxt_ref, o_ref, m_ref, acc_ref, s0_ref, s1_ref, mc0_ref, mc1_ref):
    n_heads = GQA_GROUP
    seq = kx_ref.shape[0]
    n_chunks = seq // CK
    n_qblocks = seq // TQ
    bufs = ((s0_ref, mc0_ref), (s1_ref, mc1_ref))
    assert n_chunks % 2 == 0 and sum(REGION_STEPS) == n_chunks

    def offset(i, size):
        return i * size if isinstance(i, int) else pl.multiple_of(i * size, size)

    def scores(qb, c, h, s_ref, mc_ref):
        qt = qt_ref[(h // 2) * LANES:(h // 2 + 1) * LANES, pl.ds(offset(qb, TQ), TQ)]
        kv = kx_ref[pl.ds(offset(c, CK), CK), (h % 2) * LANES:(h % 2 + 1) * LANES]
        s = jnp.dot(kv, qt, preferred_element_type=jnp.float32)
        s_ref[h] = s
        mc_ref[h] = jnp.broadcast_to(jnp.max(s, axis=0, keepdims=True), mc_ref.shape[1:])

    def accumulate(slot, c, h, s_ref, mc_ref):
        vxt = vxt_ref[0:V_ROWS, pl.ds(offset(c, CK), CK)]
        m_old = m_ref[slot, h]
        m_new = jnp.maximum(m_old, mc_ref[h])
        alpha = jnp.exp2(m_old - m_new)
        p = jnp.exp2(s_ref[h] - m_new[0:1, :])
        acc_ref[slot, h, 0:V_ROWS, :] = acc_ref[slot, h, 0:V_ROWS, :] * alpha[0:1, :] + jnp.dot(
            vxt, p, preferred_element_type=jnp.float32)
        m_ref[slot, h] = m_new

    def step(slot, qb, c, parity, qb_next, c_next):
        for h in range(n_heads):
            scores(qb_next, c_next, h, *bufs[1 - parity])
            accumulate(slot, c, h, *bufs[parity])

    def finalize(slot, qb):
        lane = jax.lax.broadcasted_iota(jnp.int32, (TQ, LANES), 1)
        for j in range(n_heads // 2):
            halves = []
            for h in (2 * j, 2 * j + 1):
                acc = acc_ref[slot, h]
                halves.append((acc * (1.0 / acc[HEAD_DIM:HEAD_DIM + 1, :])).T)
            pair = jnp.where(lane < HEAD_DIM, halves[0], pltpu.roll(halves[1], HEAD_DIM, 1))
            o_ref[pl.ds(offset(qb, TQ), TQ), j * LANES:(j + 1) * LANES] = pair.astype(o_ref.dtype)

    acc_ref[...] = jnp.ones(acc_ref.shape, jnp.float32)
    for h in range(n_heads):
        scores(0, 0, h, *bufs[0])

    def qblock(qb, carry):
        slot = qb % 2
        m_ref[slot] = jnp.full(m_ref.shape[1:], NEG_BIG, jnp.float32)

        one = jnp.minimum(qb, 0) + 1
        first = 0
        for size in REGION_STEPS:
            chunks = range(first, first + size)
            first += size

            def region(_, carry, chunks=chunks):
                for c in chunks:
                    if c == n_chunks - REGION_STEPS[-1]:
                        finalize(1 - slot, jnp.maximum(qb - 1, 0))
                    if c < n_chunks - 1:
                        step(slot, qb, c, c % 2, qb, c + 1)
                    else:
                        step(slot, qb, c, c % 2, jnp.minimum(qb + 1, n_qblocks - 1), 0)
                return carry

            jax.lax.fori_loop(0, one, region, 0)
        return carry

    jax.lax.fori_loop(0, n_qblocks, qblock, 0)
    finalize((n_qblocks - 1) % 2, n_qblocks - 1)


def _post_kernel(attn_ref, rest_ref, x_ref, wpa_ref, wpg_ref, wout_ref, ws_ref,
                 bsp_ref, fg_ref, o_ref, vmix_ref):
    f32 = jnp.float32
    tm = x_ref.shape[0]

    def rest(lo, hi):
        return rest_ref[:, lo:hi].astype(f32)

    a = attn_ref[...].astype(f32) * rest(0, ATTN_WIDTH)
    y_a = jnp.dot(a, wpa_ref[...], preferred_element_type=f32)

    o0 = ATTN_WIDTH
    vv = rest_ref[:, o0 + GMLP_WIDTH:o0 + 2 * GMLP_WIDTH]
    low = jax.lax.broadcasted_iota(jnp.int32, (CHUNK, LANES), 1) < GROUP_DIM
    zero = jnp.zeros((CHUNK, LANES), jnp.bfloat16)
    ws_pairs = [jnp.concatenate([ws_ref[2 * j], ws_ref[2 * j + 1]], axis=1).astype(jnp.bfloat16)
                for j in range(GMLP_WIDTH // LANES)]
    for c in range(tm // CHUNK):
        slabs = []
        for j in range(GMLP_WIDTH // LANES):
            v_slab = vv[c * CHUNK:(c + 1) * CHUNK, j * LANES:(j + 1) * LANES]
            rhs = jnp.concatenate([jnp.where(low, v_slab, zero), jnp.where(low, zero, v_slab)],
                                  axis=0)
            slabs.append(jnp.dot(ws_pairs[j], rhs, preferred_element_type=f32))
        vmix_ref[c * CHUNK:(c + 1) * CHUNK, :] = jnp.concatenate(slabs, axis=1) + bsp_ref[...]
    gm = rest(o0, o0 + GMLP_WIDTH) * vmix_ref[...]
    y_b = jnp.dot(gm, wpg_ref[...], preferred_element_type=f32)

    o1 = o0 + 2 * GMLP_WIDTH
    y = rest(o1, o1 + D_MODEL) * y_a + rest(o1 + D_MODEL, o1 + 2 * D_MODEL) * y_b
    out = x_ref[...] + jnp.dot(y, wout_ref[...], preferred_element_type=f32)
    ms = jnp.mean(out * out, axis=-1, keepdims=True)
    o_ref[...] = out * jax.lax.rsqrt(ms + EPS) * fg_ref[...]


def _rope_tables(seq):
    token = np.arange(seq)
    inv_freq = ROPE_THETA ** (-np.arange(0, AXIAL_DIM, 2, dtype=np.float64) / AXIAL_DIM)
    row_ang = (token // GRID_W)[:, None] * inv_freq
    col_ang = (token % GRID_W)[:, None] * inv_freq
    cos = np.concatenate([np.cos(row_ang)] * 2 + [np.cos(col_ang)] * 2, axis=-1)
    sin = np.concatenate([-np.sin(row_ang), np.sin(row_ang),
                          -np.sin(col_ang), np.sin(col_ang)], axis=-1)
    return (jnp.asarray(np.tile(cos, (1, 2)), jnp.float32),
            jnp.asarray(np.tile(sin, (1, 2)), jnp.float32))


def _layer(x2, batch, seq, norm_gain, w_in, q_gain, k_gain, w_proj_attn, ln_v_gain, ln_v_bias,
           w_spatial, b_spatial, w_proj_gmlp, b_merge, w_out, final_gain):
    n = batch * seq
    bf16 = jnp.bfloat16
    f32 = jnp.float32
    cos, sin = _rope_tables(seq)
    seg = np.arange(2 * LANES) // HEAD_DIM
    block_diag = seg[:, None] == seg[None, :]
    ones_bd = jnp.asarray(np.concatenate([block_diag, block_diag], axis=0), dtype=bf16)
    row = lambda v: v.reshape(1, -1).astype(f32)
    const = lambda shape: pl.BlockSpec(shape, lambda *_: (0,) * len(shape))
    in_width = w_in.shape[1]
    seq_blocks = seq // TM_IN

    qt, kx, vxt, rest = pl.pallas_call(
        _in_proj_kernel,
        grid=(n // TM_IN,),
        in_specs=[
            pl.BlockSpec((TM_IN, D_MODEL), lambda i: (i, 0)),
            const((1, D_MODEL)),
            pl.BlockSpec((D_MODEL, in_width), lambda i: (0, 0), pipeline_mode=pl.Buffered(1)),
            const((1, HEAD_DIM)),
            const((1, HEAD_DIM)),
            pl.BlockSpec((TM_IN, LANES), lambda i: (i % seq_blocks, 0)),
            pl.BlockSpec((TM_IN, LANES), lambda i: (i % seq_blocks, 0)),
            const((4 * LANES, 2 * LANES)),
            const((1, GMLP_WIDTH)),
            const((1, GMLP_WIDTH)),
            const((1, N_BRANCHES * D_MODEL)),
        ],
        out_specs=[
            pl.BlockSpec((ATTN_WIDTH, TM_IN), lambda i: (0, i)),
            pl.BlockSpec((TM_IN, 4 * LANES), lambda i: (i, 0)),
            pl.BlockSpec((2 * LANES, TM_IN), lambda i: (0, i)),
            pl.BlockSpec((TM_IN, REST_WIDTH), lambda i: (i, 0)),
        ],
        out_shape=[
            jax.ShapeDtypeStruct((ATTN_WIDTH, n), bf16),
            jax.ShapeDtypeStruct((n, 4 * LANES), bf16),
            jax.ShapeDtypeStruct((2 * LANES, n), f32),
            jax.ShapeDtypeStruct((n, REST_WIDTH), bf16),
        ],
        compiler_params=pltpu.CompilerParams(
            dimension_semantics=("arbitrary",), vmem_limit_bytes=VMEM_LIMIT_IN_PROJ),
        name="in_proj",
    )(x2, row(norm_gain), w_in, row(q_gain), row(k_gain),
      cos, sin, ones_bd, row(ln_v_gain), row(ln_v_bias), row(b_merge))

    kv_width = GQA_GROUP * HEAD_DIM
    attn = pl.pallas_call(
        _attn_kernel,
        grid=(batch, N_KV_HEADS),
        in_specs=[
            pl.BlockSpec((kv_width, seq), lambda b, g: (g, b)),
            pl.BlockSpec((seq, 2 * LANES), lambda b, g: (b, g)),
            pl.BlockSpec((LANES, seq), lambda b, g: (g, b)),
        ],
        out_specs=pl.BlockSpec((seq, kv_width), lambda b, g: (b, g)),
        out_shape=jax.ShapeDtypeStruct((n, ATTN_WIDTH), bf16),
        scratch_shapes=[
            pltpu.VMEM((2, GQA_GROUP, 8, TQ), f32),
            pltpu.VMEM((2, GQA_GROUP, LANES, TQ), f32),
            pltpu.VMEM((GQA_GROUP, CK, TQ), f32),
            pltpu.VMEM((GQA_GROUP, CK, TQ), f32),
            pltpu.VMEM((GQA_GROUP, 8, TQ), f32),
            pltpu.VMEM((GQA_GROUP, 8, TQ), f32),
        ],
        compiler_params=pltpu.CompilerParams(
            dimension_semantics=("arbitrary", "arbitrary"),
            vmem_limit_bytes=VMEM_LIMIT_ATTENTION),
        name="attention",
    )(qt, kx, vxt)

    b_sp = jnp.repeat(b_spatial.T, GROUP_DIM, axis=1).astype(f32)
    out = pl.pallas_call(
        _post_kernel,
        grid=(n // TM_POST,),
        in_specs=[
            pl.BlockSpec((TM_POST, ATTN_WIDTH), lambda i: (i, 0)),
            pl.BlockSpec((TM_POST, REST_WIDTH), lambda i: (i, 0)),
            pl.BlockSpec((TM_POST, D_MODEL), lambda i: (i, 0)),
            const((ATTN_WIDTH, D_MODEL)),
            const((GMLP_WIDTH, D_MODEL)),
            const((D_MODEL, D_MODEL)),
            const((N_GROUPS, CHUNK, CHUNK)),
            const((CHUNK, GMLP_WIDTH)),
            const((1, D_MODEL)),
        ],
        out_specs=pl.BlockSpec((TM_POST, D_MODEL), lambda i: (i, 0)),
        out_shape=jax.ShapeDtypeStruct((n, D_MODEL), f32),
        scratch_shapes=[pltpu.VMEM((TM_POST, GMLP_WIDTH), f32)],
        compiler_params=pltpu.CompilerParams(
            dimension_semantics=("arbitrary",), vmem_limit_bytes=VMEM_LIMIT_POST),
        name="post",
    )(attn, rest, x2, w_proj_attn, w_proj_gmlp, w_out,
      w_spatial, b_sp, row(final_gain))
    return out


def kernel(x, norm_gain, w_in, q_gain, k_gain, w_proj_attn, ln_v_gain, ln_v_bias, w_spatial,
           b_spatial, w_proj_gmlp, b_merge, w_out, final_gain):
    batch, seq, d = x.shape
    depth = w_in.shape[0]
    assert depth == 1, "the final rmsnorm is fused into the single layer's last call"
    assert d == D_MODEL and seq % TM_IN == 0 and seq % TQ == 0 and seq % CK == 0
    out = _layer(x.reshape(batch * seq, d), batch, seq, norm_gain[0], w_in[0], q_gain[0],
                 k_gain[0], w_proj_attn[0], ln_v_gain[0], ln_v_bias[0], w_spatial[0],
                 b_spatial[0], w_proj_gmlp[0], b_merge[0], w_out[0], final_gain)
    return out.reshape(batch, seq, d)
```

```python
import math

import jax
import jax.numpy as jnp
import numpy as np
from jax.experimental import pallas as pl
from jax.experimental.pallas import tpu as pltpu

D_MODEL = 1024
GRID_W = 64
HEAD_DIM = 64
N_Q_HEADS = 8
N_KV_HEADS = 2
GQA_GROUP = N_Q_HEADS // N_KV_HEADS
ATTN_WIDTH = N_Q_HEADS * HEAD_DIM
KV_WIDTH = N_KV_HEADS * HEAD_DIM
AXIAL_DIM = HEAD_DIM // 2
ROPE_THETA = 10000.0
CHUNK = 128
N_GROUPS = 8
GMLP_WIDTH = 512
GROUP_DIM = GMLP_WIDTH // N_GROUPS
N_BRANCHES = 2
EPS = 1e-6
IN_REST_WIDTH = ATTN_WIDTH + 3 * GMLP_WIDTH + N_BRANCHES * D_MODEL
REST_WIDTH = IN_REST_WIDTH - GMLP_WIDTH

LANES = 128
VMEM_LIMIT_BYTES = 56 * 1024 * 1024

TM_IN = 512
REST_STEP = 512
QKV_AFTER_CHUNKS = 5
SEGSUM_AFTER_CHUNKS = 7
TQ = 256
CK = 512
REGION_STEPS = (12, 4)
V_ROWS = 72
TM_POST = 512
REST_SLOTS = 3
NEG_BIG = -1e30


def _segment_sums(z, bd_stack):
    sq = z * z
    hi = sq.astype(jnp.bfloat16)
    lo = (sq - hi.astype(jnp.float32)).astype(jnp.bfloat16)
    return jnp.dot(jnp.concatenate([hi, lo], axis=1), bd_stack,
                   preferred_element_type=jnp.float32)


def _head_norm_rope(z, sums, gain, cos, sin_signed, scale):
    ms = sums * (1.0 / HEAD_DIM)
    y = z * (jax.lax.rsqrt(ms + EPS) * scale) * gain
    lane = jax.lax.broadcasted_iota(jnp.int32, y.shape, 1)
    half = AXIAL_DIM // 2
    partner = jnp.where((lane & half) != 0,
                        pltpu.roll(y, half, 1),
                        pltpu.roll(y, LANES - half, 1))
    return y * cos + partner * sin_signed


def _in_proj_kernel(x_ref, ng_ref, w_ref, qg_ref, kg_ref, cos_ref, sin_ref, bd_ref,
                    lng_ref, lnb_ref, bm_ref, qt_ref, kx_ref, vxt_ref, rest_ref):
    f32 = jnp.float32
    x = x_ref[...]
    ms = jnp.mean(x * x, axis=-1, keepdims=True)
    h = x * jax.lax.rsqrt(ms + EPS) * ng_ref[...]

    def project(lo, hi):
        return jnp.dot(h, w_ref[:, lo:hi], preferred_element_type=f32)

    def layernorm_v(z):
        g = jax.nn.gelu(z)
        mu = jnp.mean(g, axis=-1, keepdims=True)
        var = jnp.mean(jnp.square(g - mu), axis=-1, keepdims=True)
        return (g - mu) * jax.lax.rsqrt(var + EPS) * lng_ref[...] + lnb_ref[...]

    def merge_gate(c):
        lo = (c - 4) * REST_STEP
        return lambda z: jax.nn.sigmoid(z + bm_ref[:, lo:lo + REST_STEP])

    n_rest = IN_REST_WIDTH // REST_STEP
    activations = [jax.nn.silu, jax.nn.gelu, layernorm_v, jax.nn.silu] + [
        merge_gate(c) for c in range(4, n_rest)]
    rest_column = [0, None, 2, 1] + list(range(3, n_rest - 1))
    held = {}

    def rest_chunks(lo, hi):
        base = ATTN_WIDTH + 2 * KV_WIDTH
        for c in range(lo, hi):
            y = activations[c](project(base + c * REST_STEP, base + (c + 1) * REST_STEP))
            if c == 1:
                held["u"] = y
                continue
            if c == 3:
                y = held["u"] * y
            col = rest_column[c] * REST_STEP
            rest_ref[:, col:col + REST_STEP] = y.astype(rest_ref.dtype)

    rest_chunks(0, QKV_AFTER_CHUNKS)
    zq = project(0, ATTN_WIDTH)
    zkv = project(ATTN_WIDTH, ATTN_WIDTH + 2 * KV_WIDTH)
    rest_chunks(QKV_AFTER_CHUNKS, SEGSUM_AFTER_CHUNKS)
    bd_stack = bd_ref[...]
    sums_q = [_segment_sums(zq[:, i * 2 * LANES:(i + 1) * 2 * LANES], bd_stack)
              for i in range(ATTN_WIDTH // (2 * LANES))]
    sums_kv = _segment_sums(zkv, bd_stack)
    rest_chunks(SEGSUM_AFTER_CHUNKS, n_rest)

    cos = cos_ref[...]
    sin_signed = sin_ref[...]
    lane = jax.lax.broadcasted_iota(jnp.int32, (x.shape[0], LANES), 1)
    low = lane < HEAD_DIM
    q_gain = jnp.concatenate([qg_ref[...]] * (LANES // HEAD_DIM), axis=1)
    k_gain = jnp.concatenate([kg_ref[...]] * (LANES // HEAD_DIM), axis=1)

    scale = math.log2(math.e) / math.sqrt(HEAD_DIM)
    for j in range(ATTN_WIDTH // LANES):
        sums = sums_q[j // 2][:, (j % 2) * LANES:(j % 2 + 1) * LANES]
        q = _head_norm_rope(zq[:, j * LANES:(j + 1) * LANES], sums, q_gain, cos, sin_signed, scale)
        qt_ref[j * LANES:(j + 1) * LANES, :] = q.T.astype(qt_ref.dtype)

    k = _head_norm_rope(zkv[:, 0:LANES], sums_kv[:, 0:LANES], k_gain, cos, sin_signed, 1.0)
    k_sw = pltpu.roll(k, HEAD_DIM, 1)
    zero = jnp.zeros_like(k)
    kx_ref[:, 0 * LANES:1 * LANES] = jnp.where(low, k, zero).astype(kx_ref.dtype)
    kx_ref[:, 1 * LANES:2 * LANES] = jnp.where(low, zero, k_sw).astype(kx_ref.dtype)
    kx_ref[:, 2 * LANES:3 * LANES] = jnp.where(low, k_sw, zero).astype(kx_ref.dtype)
    kx_ref[:, 3 * LANES:4 * LANES] = jnp.where(low, zero, k).astype(kx_ref.dtype)

    v = zkv[:, LANES:2 * LANES]
    v_sw = pltpu.roll(v, HEAD_DIM, 1)
    ones_col = jnp.where(lane == HEAD_DIM, 1.0, 0.0)
    vxt_ref[0 * LANES:1 * LANES, :] = jnp.where(low, v, ones_col).T.astype(vxt_ref.dtype)
    vxt_ref[1 * LANES:2 * LANES, :] = jnp.where(low, v_sw, ones_col).T.astype(vxt_ref.dtype)


def _attn_kernel(qt_ref, kx_ref, vxt_ref, o_ref, m_ref, acc_ref, s0_ref, s1_ref, mc0_ref, mc1_ref):
    n_heads = GQA_GROUP
    seq = kx_ref.shape[0]
    n_chunks = seq // CK
    n_qblocks = seq // TQ
    bufs = ((s0_ref, mc0_ref), (s1_ref, mc1_ref))
    assert n_chunks % 2 == 0 and sum(REGION_STEPS) == n_chunks

    def offset(i, size):
        return i * size if isinstance(i, int) else pl.multiple_of(i * size, size)

    def scores(qb, c, h, s_ref, mc_ref):
        qt = qt_ref[(h // 2) * LANES:(h // 2 + 1) * LANES, pl.ds(offset(qb, TQ), TQ)]
        kv = kx_ref[pl.ds(offset(c, CK), CK), (h % 2) * LANES:(h % 2 + 1) * LANES]
        s = jnp.dot(kv, qt, preferred_element_type=jnp.float32)
        s_ref[h] = s
        mc_ref[h] = jnp.broadcast_to(jnp.max(s, axis=0, keepdims=True), mc_ref.shape[1:])

    def accumulate(slot, c, h, s_ref, mc_ref):
        vxt = vxt_ref[0:V_ROWS, pl.ds(offset(c, CK), CK)]
        m_old = m_ref[slot, h]
        m_new = jnp.maximum(m_old, mc_ref[h])
        alpha = jnp.exp2(m_old - m_new)
        p = jnp.exp2(s_ref[h] - m_new[0:1, :])
        acc_ref[slot, h, 0:V_ROWS, :] = acc_ref[slot, h, 0:V_ROWS, :] * alpha[0:1, :] + jnp.dot(
            vxt, p, preferred_element_type=jnp.float32)
        m_ref[slot, h] = m_new

    def step(slot, qb, c, parity, qb_next, c_next):
        for h in range(n_heads):
            scores(qb_next, c_next, h, *bufs[1 - parity])
            accumulate(slot, c, h, *bufs[parity])

    def finalize(slot, qb):
        lane = jax.lax.broadcasted_iota(jnp.int32, (TQ, LANES), 1)
        for j in range(n_heads // 2):
            halves = []
            for h in (2 * j, 2 * j + 1):
                acc = acc_ref[slot, h]
                halves.append((acc * (1.0 / acc[HEAD_DIM:HEAD_DIM + 1, :])).T)
            pair = jnp.where(lane < HEAD_DIM, halves[0], pltpu.roll(halves[1], HEAD_DIM, 1))
            o_ref[pl.ds(offset(qb, TQ), TQ), j * LANES:(j + 1) * LANES] = pair.astype(o_ref.dtype)

    acc_ref[...] = jnp.ones(acc_ref.shape, jnp.float32)
    for h in range(n_heads):
        scores(0, 0, h, *bufs[0])

    def qblock(qb, carry):
        slot = qb % 2
        m_ref[slot] = jnp.full(m_ref.shape[1:], NEG_BIG, jnp.float32)

        one = jnp.minimum(qb, 0) + 1
        first = 0
        for size in REGION_STEPS:
            chunks = range(first, first + size)
            first += size

            def region(_, carry, chunks=chunks):
                for c in chunks:
                    if c == n_chunks - REGION_STEPS[-1]:
                        finalize(1 - slot, jnp.maximum(qb - 1, 0))
                    if c < n_chunks - 1:
                        step(slot, qb, c, c % 2, qb, c + 1)
                    else:
                        step(slot, qb, c, c % 2, jnp.minimum(qb + 1, n_qblocks - 1), 0)
                return carry

            jax.lax.fori_loop(0, one, region, 0)
        return carry

    jax.lax.fori_loop(0, n_qblocks, qblock, 0)
    finalize((n_qblocks - 1) % 2, n_qblocks - 1)


def _post_kernel(attn_ref, rest_hbm, x_ref, wpa_ref, wpg_ref, wout_ref, ws_ref,
                 bsp_ref, fg_ref, o_ref, vmix_ref, rest_buf, rest_sem):
    f32 = jnp.float32
    tm = x_ref.shape[0]
    i = pl.program_id(0)
    n_steps = pl.num_programs(0)
    ahead = REST_SLOTS - 1

    def rest_copy(step):
        slot = step % REST_SLOTS
        return pltpu.make_async_copy(
            rest_hbm.at[pl.ds(pl.multiple_of(step * tm, tm), tm), :],
            rest_buf.at[slot], rest_sem.at[slot])

    @pl.when(i == 0)
    def _():
        for s in range(ahead):
            rest_copy(s).start()

    @pl.when(i + ahead < n_steps)
    def _():
        rest_copy(i + ahead).start()

    rest_copy(i).wait()
    rest_ref = rest_buf.at[i % REST_SLOTS]

    def rest(lo, hi):
        return rest_ref[:, lo:hi].astype(f32)

    a = attn_ref[...].astype(f32) * rest(0, ATTN_WIDTH)
    y_a = jnp.dot(a, wpa_ref[...], preferred_element_type=f32)

    o0 = ATTN_WIDTH
    vv = rest_ref[:, o0 + GMLP_WIDTH:o0 + 2 * GMLP_WIDTH]
    low = jax.lax.broadcasted_iota(jnp.int32, (CHUNK, LANES), 1) < GROUP_DIM
    zero = jnp.zeros((CHUNK, LANES), jnp.bfloat16)
    ws_pairs = [jnp.concatenate([ws_ref[2 * j], ws_ref[2 * j + 1]], axis=1).astype(jnp.bfloat16)
                for j in range(GMLP_WIDTH // LANES)]
    for c in range(tm // CHUNK):
        slabs = []
        for j in range(GMLP_WIDTH // LANES):
            v_slab = vv[c * CHUNK:(c + 1) * CHUNK, j * LANES:(j + 1) * LANES]
            rhs = jnp.concatenate([jnp.where(low, v_slab, zero), jnp.where(low, zero, v_slab)],
                                  axis=0)
            slabs.append(jnp.dot(ws_pairs[j], rhs, preferred_element_type=f32))
        vmix_ref[c * CHUNK:(c + 1) * CHUNK, :] = jnp.concatenate(slabs, axis=1) + bsp_ref[...]
    gm = rest(o0, o0 + GMLP_WIDTH) * vmix_ref[...]
    y_b = jnp.dot(gm, wpg_ref[...], preferred_element_type=f32)

    o1 = o0 + 2 * GMLP_WIDTH
    y = rest(o1, o1 + D_MODEL) * y_a + rest(o1 + D_MODEL, o1 + 2 * D_MODEL) * y_b
    out = x_ref[...] + jnp.dot(y, wout_ref[...], preferred_element_type=f32)
    ms = jnp.mean(out * out, axis=-1, keepdims=True)
    o_ref[...] = out * jax.lax.rsqrt(ms + EPS) * fg_ref[...]


def _rope_tables(seq):
    token = np.arange(seq)
    inv_freq = ROPE_THETA ** (-np.arange(0, AXIAL_DIM, 2, dtype=np.float64) / AXIAL_DIM)
    row_ang = (token // GRID_W)[:, None] * inv_freq
    col_ang = (token % GRID_W)[:, None] * inv_freq
    cos = np.concatenate([np.cos(row_ang)] * 2 + [np.cos(col_ang)] * 2, axis=-1)
    sin = np.concatenate([-np.sin(row_ang), np.sin(row_ang),
                          -np.sin(col_ang), np.sin(col_ang)], axis=-1)
    return (jnp.asarray(np.tile(cos, (1, 2)), jnp.float32),
            jnp.asarray(np.tile(sin, (1, 2)), jnp.float32))


def _layer(x2, batch, seq, norm_gain, w_in, q_gain, k_gain, w_proj_attn, ln_v_gain, ln_v_bias,
           w_spatial, b_spatial, w_proj_gmlp, b_merge, w_out, final_gain):
    n = batch * seq
    bf16 = jnp.bfloat16
    f32 = jnp.float32
    cos, sin = _rope_tables(seq)
    seg = np.arange(2 * LANES) // HEAD_DIM
    block_diag = seg[:, None] == seg[None, :]
    ones_bd = jnp.asarray(np.concatenate([block_diag, block_diag], axis=0), dtype=bf16)
    row = lambda v: v.reshape(1, -1).astype(f32)
    const = lambda shape: pl.BlockSpec(shape, lambda *_: (0,) * len(shape))
    in_width = w_in.shape[1]
    seq_blocks = seq // TM_IN

    qt, kx, vxt, rest = pl.pallas_call(
        _in_proj_kernel,
        grid=(n // TM_IN,),
        in_specs=[
            pl.BlockSpec((TM_IN, D_MODEL), lambda i: (i, 0)),
            const((1, D_MODEL)),
            pl.BlockSpec((D_MODEL, in_width), lambda i: (0, 0), pipeline_mode=pl.Buffered(1)),
            const((1, HEAD_DIM)),
            const((1, HEAD_DIM)),
            pl.BlockSpec((TM_IN, LANES), lambda i: (i % seq_blocks, 0)),
            pl.BlockSpec((TM_IN, LANES), lambda i: (i % seq_blocks, 0)),
            const((4 * LANES, 2 * LANES)),
            const((1, GMLP_WIDTH)),
            const((1, GMLP_WIDTH)),
            const((1, N_BRANCHES * D_MODEL)),
        ],
        out_specs=[
            pl.BlockSpec((ATTN_WIDTH, TM_IN), lambda i: (0, i)),
            pl.BlockSpec((TM_IN, 4 * LANES), lambda i: (i, 0)),
            pl.BlockSpec((2 * LANES, TM_IN), lambda i: (0, i)),
            pl.BlockSpec((TM_IN, REST_WIDTH), lambda i: (i, 0)),
        ],
        out_shape=[
            jax.ShapeDtypeStruct((ATTN_WIDTH, n), bf16),
            jax.ShapeDtypeStruct((n, 4 * LANES), bf16),
            jax.ShapeDtypeStruct((2 * LANES, n), f32),
            jax.ShapeDtypeStruct((n, REST_WIDTH), bf16),
        ],
        compiler_params=pltpu.CompilerParams(
            dimension_semantics=("arbitrary",), vmem_limit_bytes=VMEM_LIMIT_BYTES),
        name="in_proj",
    )(x2, row(norm_gain), w_in, row(q_gain), row(k_gain),
      cos, sin, ones_bd, row(ln_v_gain), row(ln_v_bias), row(b_merge))

    kv_width = GQA_GROUP * HEAD_DIM
    attn = pl.pallas_call(
        _attn_kernel,
        grid=(batch, N_KV_HEADS),
        in_specs=[
            pl.BlockSpec((kv_width, seq), lambda b, g: (g, b)),
            pl.BlockSpec((seq, 2 * LANES), lambda b, g: (b, g)),
            pl.BlockSpec((LANES, seq), lambda b, g: (g, b)),
        ],
        out_specs=pl.BlockSpec((seq, kv_width), lambda b, g: (b, g)),
        out_shape=jax.ShapeDtypeStruct((n, ATTN_WIDTH), bf16),
        scratch_shapes=[
            pltpu.VMEM((2, GQA_GROUP, 8, TQ), f32),
            pltpu.VMEM((2, GQA_GROUP, LANES, TQ), f32),
            pltpu.VMEM((GQA_GROUP, CK, TQ), f32),
            pltpu.VMEM((GQA_GROUP, CK, TQ), f32),
            pltpu.VMEM((GQA_GROUP, 8, TQ), f32),
            pltpu.VMEM((GQA_GROUP, 8, TQ), f32),
        ],
        compiler_params=pltpu.CompilerParams(
            dimension_semantics=("arbitrary", "arbitrary"),
            vmem_limit_bytes=VMEM_LIMIT_BYTES),
        name="attention",
    )(qt, kx, vxt)

    b_sp = jnp.repeat(b_spatial.T, GROUP_DIM, axis=1).astype(f32)
    out = pl.pallas_call(
        _post_kernel,
        grid=(n // TM_POST,),
        in_specs=[
            pl.BlockSpec((TM_POST, ATTN_WIDTH), lambda i: (i, 0)),
            pl.BlockSpec(memory_space=pl.ANY),
            pl.BlockSpec((TM_POST, D_MODEL), lambda i: (i, 0)),
            const((ATTN_WIDTH, D_MODEL)),
            const((GMLP_WIDTH, D_MODEL)),
            const((D_MODEL, D_MODEL)),
            const((N_GROUPS, CHUNK, CHUNK)),
            const((CHUNK, GMLP_WIDTH)),
            const((1, D_MODEL)),
        ],
        out_specs=pl.BlockSpec((TM_POST, D_MODEL), lambda i: (i, 0)),
        out_shape=jax.ShapeDtypeStruct((n, D_MODEL), f32),
        scratch_shapes=[pltpu.VMEM((TM_POST, GMLP_WIDTH), f32),
                        pltpu.VMEM((REST_SLOTS, TM_POST, REST_WIDTH), bf16),
                        pltpu.SemaphoreType.DMA((REST_SLOTS,))],
        compiler_params=pltpu.CompilerParams(
            dimension_semantics=("arbitrary",), vmem_limit_bytes=VMEM_LIMIT_BYTES),
        name="post",
    )(attn, rest, x2, w_proj_attn, w_proj_gmlp, w_out,
      w_spatial, b_sp, row(final_gain))
    return out


def kernel(x, norm_gain, w_in, q_gain, k_gain, w_proj_attn, ln_v_gain, ln_v_bias, w_spatial,
           b_spatial, w_proj_gmlp, b_merge, w_out, final_gain):
    batch, seq, d = x.shape
    depth = w_in.shape[0]
    assert depth == 1, "the final rmsnorm is fused into the single layer's last call"
    assert d == D_MODEL and seq % TM_IN == 0 and seq % TQ == 0 and seq % CK == 0
    assert (batch * seq) % TM_POST == 0 and batch * seq // TM_POST >= REST_SLOTS
    out = _layer(x.reshape(batch * seq, d), batch, seq, norm_gain[0], w_in[0], q_gain[0],
                 k_gain[0], w_proj_attn[0], ln_v_gain[0], ln_v_bias[0], w_spatial[0],
                 b_spatial[0], w_proj_gmlp[0], b_merge[0], w_out[0], final_gain)
    return out.reshape(batch, seq, d)
```
